```python
import math
import jax, jax.numpy as jnp
from jax import lax
import numpy as np

D_MODEL = 1024
BATCH = 2
SEQ = 8192
DEPTH = 2

D_MIX = D_MODEL
D_CONV = D_MIX // 4
D_ATTN = D_MIX // 2
D_SGU = D_MIX // 4
N_ATTN_HEADS = 4
ATTN_V_DIM = D_ATTN // N_ATTN_HEADS
ATTN_QK_DIM = ATTN_V_DIM // 2
Q_BLOCK = 128
CONV_WIDTH = 31
N_SGU_GROUPS = 4
SGU_GROUP_DIM = D_SGU // N_SGU_GROUPS
CHUNK = 128
D_IN = 2 * D_CONV + 3 * D_ATTN + 2 * D_SGU + D_MIX
EPS = 1e-6
NEG_INF = -1e30

kernel_name = "hybrid_conv_diffattn_sgu_block"


def _rms_norm(x, g):
    xf = x.astype(jnp.float32)
    y = xf * lax.rsqrt(jnp.mean(xf * xf, axis=-1, keepdims=True) + EPS)
    return (y * g.astype(jnp.float32)).astype(x.dtype)


def _layer_norm(x, g, b):
    xf = x.astype(jnp.float32)
    mu = jnp.mean(xf, axis=-1, keepdims=True)
    xc = xf - mu
    y = xc * lax.rsqrt(jnp.mean(xc * xc, axis=-1, keepdims=True) + EPS)
    return (y * g.astype(jnp.float32) + b.astype(jnp.float32)).astype(x.dtype)


def _conv_branch(h, conv_w, conv_b, ln_g, ln_b):
    a, gt = jnp.split(h, 2, axis=-1)
    z = a * jax.nn.sigmoid(gt)
    z = lax.conv_general_dilated(
        z, conv_w[:, None, :].astype(z.dtype),
        window_strides=(1,), padding=[(CONV_WIDTH - 1, 0)],
        dimension_numbers=("NWC", "WIO", "NWC"),
        feature_group_count=D_CONV) + conv_b
    z = _layer_norm(z, ln_g, ln_b)
    return jax.nn.silu(z)


def _diff_attention(q, k, v, lam, lam_init, subln_g):
    B, S, _ = q.shape
    q = q.reshape(B, S, N_ATTN_HEADS, 2, ATTN_QK_DIM)
    k = k.reshape(B, S, N_ATTN_HEADS, 2, ATTN_QK_DIM)
    v = v.reshape(B, S, N_ATTN_HEADS, ATTN_V_DIM)
    scale = 1.0 / math.sqrt(ATTN_QK_DIM)
    slopes = jnp.exp2(-8.0 * (jnp.arange(N_ATTN_HEADS, dtype=jnp.float32) + 1.0) / N_ATTN_HEADS)
    kpos = jnp.arange(S)
    n_blocks = S // Q_BLOCK

    def block(i):
        qb = lax.dynamic_slice_in_dim(q, i * Q_BLOCK, Q_BLOCK, axis=1)
        s = jnp.einsum('bqhcd,bkhcd->bhcqk', qb, k).astype(jnp.float32) * scale
        qpos = i * Q_BLOCK + jnp.arange(Q_BLOCK)
        rel = qpos[:, None] - kpos[None, :]
        alibi = -slopes[:, None, None] * rel.astype(jnp.float32)[None]
        s = jnp.where((rel >= 0)[None, None, None], s + alibi[None, :, None], NEG_INF)
        p = jax.nn.softmax(s, axis=-1)
        pd = p[:, :, 0] - lam * p[:, :, 1]
        return jnp.einsum('bhqk,bkhv->bqhv', pd.astype(v.dtype), v)

    o = lax.map(block, jnp.arange(n_blocks))
    o = jnp.transpose(o, (1, 0, 2, 3, 4)).reshape(B, S, N_ATTN_HEADS, ATTN_V_DIM)
    o = _rms_norm(o, subln_g) * (1.0 - lam_init)
    return o.reshape(B, S, D_ATTN)


def _sgu_branch(h, ln_g, ln_b, w_s, b_s):
    B, S, _ = h.shape
    u, vv = jnp.split(h, 2, axis=-1)
    vv = _layer_norm(vv, ln_g, ln_b)
    vv = vv.reshape(B, S // CHUNK, CHUNK, N_SGU_GROUPS, SGU_GROUP_DIM)
    w = w_s * jnp.tril(jnp.ones((CHUNK, CHUNK), w_s.dtype))[None]
    mixed = jnp.einsum('gts,bcsgd->bctgd', w, vv) + b_s.T[None, None, :, :, None]
    return u * mixed.reshape(B, S, D_SGU)


def setup_inputs(seed: int = 0) -> dict:
    key = jax.random.key(seed)
    ks = jax.random.split(key, 20)
    f32 = jnp.float32
    n = lambda k, shape: jax.random.normal(k, shape, f32)
    return {
        "x": n(ks[0], (BATCH, SEQ, D_MODEL)),
        "norm_g": 1.0 + 0.02 * n(ks[1], (DEPTH, D_MODEL)),
        "w_in": n(ks[2], (DEPTH, D_MODEL, D_IN)) * D_MODEL ** -0.5,
        "conv_w": n(ks[3], (DEPTH, CONV_WIDTH, D_CONV)) * CONV_WIDTH ** -0.5,
        "conv_b": 0.02 * n(ks[4], (DEPTH, D_CONV)),
        "conv_ln_g": 1.0 + 0.02 * n(ks[5], (DEPTH, D_CONV)),
        "conv_ln_b": 0.02 * n(ks[6], (DEPTH, D_CONV)),
        "lam_q1": 0.1 * n(ks[7], (DEPTH, ATTN_QK_DIM)),
        "lam_k1": 0.1 * n(ks[8], (DEPTH, ATTN_QK_DIM)),
        "lam_q2": 0.1 * n(ks[9], (DEPTH, ATTN_QK_DIM)),
        "lam_k2": 0.1 * n(ks[10], (DEPTH, ATTN_QK_DIM)),
        "subln_g": 1.0 + 0.02 * n(ks[11], (DEPTH, ATTN_V_DIM)),
        "sgu_ln_g": 1.0 + 0.02 * n(ks[12], (DEPTH, D_SGU)),
        "sgu_ln_b": 0.02 * n(ks[13], (DEPTH, D_SGU)),
        "w_s": n(ks[14], (DEPTH, N_SGU_GROUPS, CHUNK, CHUNK)) * CHUNK ** -0.5,
        "b_s": 1.0 + 0.02 * n(ks[15], (DEPTH, N_SGU_GROUPS, CHUNK)),
        "w_out": n(ks[16], (DEPTH, D_MIX, D_MODEL)) * D_MIX ** -0.5,
        "norm_f": 1.0 + 0.02 * n(ks[17], (D_MODEL,)),
    }


def reference(x, norm_g, w_in, conv_w, conv_b, conv_ln_g, conv_ln_b,
              lam_q1, lam_k1, lam_q2, lam_k2, subln_g,
              sgu_ln_g, sgu_ln_b, w_s, b_s, w_out, norm_f):
    splits = np.cumsum([2 * D_CONV, D_ATTN, D_ATTN, D_ATTN, 2 * D_SGU]).tolist()
    for l in range(DEPTH):
        lam_init = 0.8 - 0.6 * math.exp(-0.3 * l)
        lam = (jnp.exp(jnp.sum(lam_q1[l].astype(jnp.float32) * lam_k1[l].astype(jnp.float32)))
               - jnp.exp(jnp.sum(lam_q2[l].astype(jnp.float32) * lam_k2[l].astype(jnp.float32)))
               + lam_init)
        h = _rms_norm(x, norm_g[l])
        proj = jnp.einsum('bsd,de->bse', h, w_in[l])
        p_conv, p_q, p_k, p_v, p_sgu, gate = jnp.split(proj, splits, axis=-1)
        y_a = _conv_branch(p_conv, conv_w[l], conv_b[l], conv_ln_g[l], conv_ln_b[l])
        y_b = _diff_attention(p_q, p_k, p_v, lam, lam_init, subln_g[l])
        y_c = _sgu_branch(p_sgu, sgu_ln_g[l], sgu_ln_b[l], w_s[l], b_s[l])
        y = jnp.concatenate([y_a, y_b, y_c], axis=-1) * jax.nn.silu(gate)
        x = x + jnp.einsum('bse,ed->bsd', y, w_out[l])
    return _rms_norm(x, norm_f)
```

```python
import functools
import math

import jax
import jax.numpy as jnp
from jax import lax
from jax.experimental import pallas as pl
from jax.experimental.pallas import tpu as pltpu

D_MODEL = 1024
D_MIX = D_MODEL
D_CONV = D_MIX // 4
D_ATTN = D_MIX // 2
D_SGU = D_MIX // 4
N_HEADS = 4
V_DIM = D_ATTN // N_HEADS
QK_DIM = V_DIM // 2
CONV_WIDTH = 31
N_GROUPS = 4
GROUP_DIM = D_SGU // N_GROUPS
CHUNK = 128
D_IN = 2 * D_CONV + 3 * D_ATTN + 2 * D_SGU + D_MIX
EPS = 1e-6
NEG = -1e30

C_CONV = 0
C_Q = C_CONV + 2 * D_CONV
C_K = C_Q + D_ATTN
C_V = C_K + D_ATTN
C_SGU = C_V + D_ATTN
C_GATE = C_SGU + 2 * D_SGU

TM = 512
TQ = 512
TK = 512
HALO = 32
VMEM_LIMIT = 56 * 1024 * 1024

F32 = jnp.float32
BF16 = jnp.bfloat16


def _in_proj_kernel(x_ref, g_ref, w_ref, pconv_ref, q_ref, k_ref, vt_ref, psgu_ref, gate_ref):
    x = x_ref[0]
    ms = jnp.mean(x * x, axis=-1, keepdims=True)
    h = ((x * lax.rsqrt(ms + EPS)) * g_ref[...]).astype(BF16)

    def proj(lo, width):
        return jnp.dot(h, w_ref[:, lo:lo + width], preferred_element_type=F32)

    pconv_ref[0] = proj(C_CONV, 2 * D_CONV)
    q_ref[0] = (proj(C_Q, D_ATTN) * (1.0 / math.sqrt(QK_DIM))).astype(BF16)
    k_ref[0] = proj(C_K, D_ATTN).astype(BF16)
    v = proj(C_V, D_ATTN)
    for hd in range(N_HEADS):
        for c in range(TM // TK):
            blk = v[c * TK:(c + 1) * TK, hd * V_DIM:(hd + 1) * V_DIM]
            vt_ref[0, hd, c] = blk.T.astype(BF16)
    psgu_ref[0] = proj(C_SGU, 2 * D_SGU)
    gate_ref[0] = proj(C_GATE, D_MIX)


def _in_proj(x, g, w):
    B, S, D = x.shape
    n_k = S // TK
    grid = (B, S // TM)
    row = lambda width: pl.BlockSpec((1, TM, width), lambda b, i: (b, i, 0))
    return pl.pallas_call(
        _in_proj_kernel,
        grid=grid,
        in_specs=[
            row(D),
            pl.BlockSpec((1, D), lambda b, i: (0, 0)),
            pl.BlockSpec((D, D_IN), lambda b, i: (0, 0)),
        ],
        out_specs=[
            row(2 * D_CONV),
            row(D_ATTN),
            row(D_ATTN),
            pl.BlockSpec((1, N_HEADS, TM // TK, V_DIM, TK), lambda b, i: (b, 0, i, 0, 0)),
            row(2 * D_SGU),
            row(D_MIX),
        ],
        out_shape=[
            jax.ShapeDtypeStruct((B, S, 2 * D_CONV), F32),
            jax.ShapeDtypeStruct((B, S, D_ATTN), BF16),
            jax.ShapeDtypeStruct((B, S, D_ATTN), BF16),
            jax.ShapeDtypeStruct((B, N_HEADS, n_k, V_DIM, TK), BF16),
            jax.ShapeDtypeStruct((B, S, 2 * D_SGU), F32),
            jax.ShapeDtypeStruct((B, S, D_MIX), F32),
        ],
        compiler_params=pltpu.CompilerParams(
            dimension_semantics=("arbitrary", "arbitrary"),
            vmem_limit_bytes=VMEM_LIMIT),
        name="in_proj",
    )(x, g, w)


def _attn_kernel(lam_init, q_ref, k_ref, vt_ref, slope_ref, lq1_ref, lk1_ref, lq2_ref, lk2_ref,
                 g_ref, o_ref, qt_ref, alibi_ref, m_ref, l_ref, acc_ref):
    i = pl.program_id(2)
    slope = slope_ref[0][:, :1]

    qf = q_ref[0].astype(F32).T
    sub = lax.broadcasted_iota(jnp.int32, (V_DIM, TQ), 0)
    qt_ref[:, :TQ] = jnp.where(sub < QK_DIM, qf, 0.0).astype(BF16)
    qt_ref[:, TQ:] = jnp.where(sub >= QK_DIM, qf, 0.0).astype(BF16)

    key_pos = lax.broadcasted_iota(jnp.int32, (TK, 2 * TQ), 0).astype(F32)
    alibi_ref[...] = key_pos * slope

    m_ref[...] = jnp.full(m_ref.shape, NEG, F32)
    l_ref[...] = jnp.zeros(l_ref.shape, F32)
    acc_ref[...] = jnp.zeros(acc_ref.shape, F32)

    def step(j, masked):
        kb = k_ref[0, pl.ds(pl.multiple_of(j * TK, TK), TK), :]
        st = jnp.dot(kb, qt_ref[...], preferred_element_type=F32)
        st = st + alibi_ref[...]
        if masked:
            kr = lax.broadcasted_iota(jnp.int32, (TK, 2 * TQ), 0)
            qr = lax.broadcasted_iota(jnp.int32, (TK, 2 * TQ), 1)
            qr = jnp.where(qr >= TQ, qr - TQ, qr)
            st = jnp.where(kr <= qr, st, NEG)
        c = slope * ((j - i) * TK).astype(F32)
        m_old = m_ref[...]
        m_new = jnp.maximum(m_old, jnp.max(st, axis=0, keepdims=True) + c)
        alpha = jnp.exp(m_old - m_new)
        p = jnp.exp(st - (m_new - c))
        l_ref[...] = alpha * l_ref[...] + jnp.sum(p, axis=0, keepdims=True)
        pv = jnp.dot(vt_ref[0, 0, j], p.astype(BF16), preferred_element_type=F32)
        acc_ref[...] = alpha * acc_ref[...] + pv
        m_ref[...] = m_new

    def body(j, carry):
        step(j, False)
        return carry

    lax.fori_loop(0, i, body, 0)
    step(i, True)

    o = acc_ref[...] / l_ref[...]
    lam = (jnp.exp(jnp.sum(lq1_ref[...] * lk1_ref[...], axis=-1, keepdims=True))
           - jnp.exp(jnp.sum(lq2_ref[...] * lk2_ref[...], axis=-1, keepdims=True))
           + lam_init)
    od = o[:, :TQ] - lam * o[:, TQ:]
    ms = jnp.mean(od * od, axis=0, keepdims=True)
    y = (od * lax.rsqrt(ms + EPS)).T
    o_ref[0] = (y * g_ref[...]) * (1.0 - lam_init)


def _attention(q, k, vt, slopes, lq1, lk1, lq2, lk2, subln_g, lam_init):
    B, S, _ = q.shape
    n_k = S // TK
    grid = (B, N_HEADS, S // TQ)
    small = lambda width: pl.BlockSpec((1, width), lambda b, h, i: (0, 0))
    return pl.pallas_call(
        functools.partial(_attn_kernel, lam_init),
        grid=grid,
        in_specs=[
            pl.BlockSpec((1, TQ, V_DIM), lambda b, h, i: (b, i, h)),
            pl.BlockSpec((1, S, V_DIM), lambda b, h, i: (b, 0, h)),
            pl.BlockSpec((1, 1, n_k, V_DIM, TK), lambda b, h, i: (b, h, 0, 0, 0)),
            pl.BlockSpec((1, 1, 128), lambda b, h, i: (h, 0, 0)),
            small(QK_DIM), small(QK_DIM), small(QK_DIM), small(QK_DIM),
            small(V_DIM),
        ],
        out_specs=pl.BlockSpec((1, TQ, V_DIM), lambda b, h, i: (b, i, h)),
        out_shape=jax.ShapeDtypeStruct((B, S, D_ATTN), F32),
        scratch_shapes=[
            pltpu.VMEM((V_DIM, 2 * TQ), BF16),
            pltpu.VMEM((TK, 2 * TQ), F32),
            pltpu.VMEM((1, 2 * TQ), F32),
            pltpu.VMEM((1, 2 * TQ), F32),
            pltpu.VMEM((V_DIM, 2 * TQ), F32),
        ],
        compiler_params=pltpu.CompilerParams(
            dimension_semantics=("arbitrary", "arbitrary", "arbitrary"),
            vmem_limit_bytes=VMEM_LIMIT),
        name="diff_attn",
    )(q, k, vt, slopes, lq1, lk1, lq2, lk2, subln_g)


def _layer_norm(v, g, b):
    mu = jnp.mean(v, axis=-1, keepdims=True)
    vc = v - mu
    var = jnp.mean(vc * vc, axis=-1, keepdims=True)
    return (vc * lax.rsqrt(var + EPS)) * g + b


def _silu(v):
    return v * jax.nn.sigmoid(v)


def _mix_kernel(final, x_ref, pconv_ref, halo_ref, yb_ref, psgu_ref, gate_ref,
                cw_ref, cb_ref, clg_ref, clb_ref, slg_ref, slb_ref, ws_ref, bs_ref,
                wout_ref, nf_ref, o_ref, zbuf_ref, ybuf_ref):
    i = pl.program_id(1)

    pc = pconv_ref[0]
    zbuf_ref[HALO:, :] = pc[:, :D_CONV] * jax.nn.sigmoid(pc[:, D_CONV:])
    ph = halo_ref[0]
    zh = ph[:, :D_CONV] * jax.nn.sigmoid(ph[:, D_CONV:])
    zbuf_ref[:HALO, :] = jnp.where(i > 0, zh, 0.0)
    conv = jnp.zeros((TM, D_CONV), F32) + cb_ref[...]
    base = HALO - (CONV_WIDTH - 1)
    for w in range(CONV_WIDTH):
        conv = conv + zbuf_ref[pl.ds(base + w, TM), :] * cw_ref[w:w + 1, :]
    y_a = _silu(_layer_norm(conv, clg_ref[...], clb_ref[...]))
    gate = gate_ref[0]
    ybuf_ref[:, :D_CONV] = (y_a * _silu(gate[:, :D_CONV])).astype(BF16)

    ybuf_ref[:, D_CONV:D_CONV + D_ATTN] = (
        yb_ref[0] * _silu(gate[:, D_CONV:D_CONV + D_ATTN])).astype(BF16)

    ps = psgu_ref[0]
    u = ps[:, :D_SGU]
    vv = _layer_norm(ps[:, D_SGU:], slg_ref[...], slb_ref[...]).astype(BF16)
    tr = lax.broadcasted_iota(jnp.int32, (CHUNK, CHUNK), 0)
    tc = lax.broadcasted_iota(jnp.int32, (CHUNK, CHUNK), 1)
    wm = [jnp.where(tr >= tc, ws_ref[g], 0.0).astype(BF16) for g in range(N_GROUPS)]
    lane_group = lax.broadcasted_iota(jnp.int32, (CHUNK, D_SGU), 1) // GROUP_DIM
    gate_c = _silu(gate[:, D_CONV + D_ATTN:])
    for c in range(TM // CHUNK):
        rows = slice(c * CHUNK, (c + 1) * CHUNK)
        vc = vv[rows, :]
        mixed = jnp.zeros((CHUNK, D_SGU), F32)
        for g in range(N_GROUPS):
            r = jnp.dot(wm[g], vc, preferred_element_type=F32)
            mixed = jnp.where(lane_group == g, r, mixed)
        y_c = u[rows, :] * (mixed + bs_ref[...])
        ybuf_ref[rows, D_CONV + D_ATTN:] = (y_c * gate_c[rows, :]).astype(BF16)

    out = x_ref[0] + jnp.dot(ybuf_ref[...], wout_ref[...], preferred_element_type=F32)
    if final:
        ms = jnp.mean(out * out, axis=-1, keepdims=True)
        out = (out * lax.rsqrt(ms + EPS)) * nf_ref[...]
    o_ref[0] = out


def _mix(final, x, pconv, yb, psgu, gate, cw, cb, clg, clb, slg, slb, ws, bs_exp, wout, nf):
    B, S, D = x.shape
    grid = (B, S // TM)
    row = lambda width: pl.BlockSpec((1, TM, width), lambda b, i: (b, i, 0))
    vec = lambda width: pl.BlockSpec((1, width), lambda b, i: (0, 0))
    halo_blocks = TM // HALO
    return pl.pallas_call(
        functools.partial(_mix_kernel, final),
        grid=grid,
        in_specs=[
            row(D),
            row(2 * D_CONV),
            pl.BlockSpec((1, HALO, 2 * D_CONV),
                         lambda b, i: (b, jnp.maximum(i * halo_blocks - 1, 0), 0)),
            row(D_ATTN),
            row(2 * D_SGU),
            row(D_MIX),
            pl.BlockSpec((CONV_WIDTH, D_CONV), lambda b, i: (0, 0)),
            vec(D_CONV), vec(D_CONV), vec(D_CONV),
            vec(D_SGU), vec(D_SGU),
            pl.BlockSpec((N_GROUPS, CHUNK, CHUNK), lambda b, i: (0, 0, 0)),
            pl.BlockSpec((CHUNK, D_SGU), lambda b, i: (0, 0)),
            pl.BlockSpec((D_MIX, D), lambda b, i: (0, 0)),
            vec(D),
        ],
        out_specs=row(D),
        out_shape=jax.ShapeDtypeStruct((B, S, D), F32),
        scratch_shapes=[
            pltpu.VMEM((TM + HALO, D_CONV), F32),
            pltpu.VMEM((TM, D_MIX), BF16),
        ],
        compiler_params=pltpu.CompilerParams(
            dimension_semantics=("arbitrary", "arbitrary"),
            vmem_limit_bytes=VMEM_LIMIT),
        name="mix_final" if final else "mix",
    )(x, pconv, pconv, yb, psgu, gate, cw, cb, clg, clb, slg, slb, ws, bs_exp, wout, nf)


def kernel(x, norm_g, w_in, conv_w, conv_b, conv_ln_g, conv_ln_b, lam_q1, lam_k1, lam_q2, lam_k2,
           subln_g, sgu_ln_g, sgu_ln_b, w_s, b_s, w_out, norm_f):
    depth = w_in.shape[0]
    slopes = jnp.exp2(-8.0 * (jnp.arange(N_HEADS, dtype=F32) + 1.0) / N_HEADS)
    slopes = jnp.broadcast_to(slopes[:, None, None], (N_HEADS, 1, 128))
    r2 = lambda a: a.reshape(1, -1)
    for l in range(depth):
        lam_init = 0.8 - 0.6 * math.exp(-0.3 * l)
        pconv, q, k, vt, psgu, gate = _in_proj(x, r2(norm_g[l]), w_in[l].astype(BF16))
        yb = _attention(q, k, vt, slopes, r2(lam_q1[l]), r2(lam_k1[l]), r2(lam_q2[l]),
                        r2(lam_k2[l]), r2(subln_g[l]), lam_init)
        bs_exp = jnp.repeat(b_s[l].T, GROUP_DIM, axis=1)
        x = _mix(l == depth - 1, x, pconv, yb, psgu, gate, conv_w[l], r2(conv_b[l]),
                 r2(conv_ln_g[l]), r2(conv_ln_b[l]), r2(sgu_ln_g[l]), r2(sgu_ln_b[l]),
                 w_s[l], bs_exp, w_out[l].astype(BF16), r2(norm_f))
    return x
```

```python
import functools
import math

import jax
import jax.numpy as jnp
from jax import lax
from jax.experimental import pallas as pl
from jax.experimental.pallas import tpu as pltpu

D_MODEL = 1024
D_MIX = D_MODEL
D_CONV = D_MIX // 4
D_ATTN = D_MIX // 2
D_SGU = D_MIX // 4
N_HEADS = 4
V_DIM = D_ATTN // N_HEADS
QK_DIM = V_DIM // 2
CONV_WIDTH = 31
N_GROUPS = 4
GROUP_DIM = D_SGU // N_GROUPS
CHUNK = 128
D_IN = 2 * D_CONV + 3 * D_ATTN + 2 * D_SGU + D_MIX
EPS = 1e-6
NEG = -1e30

C_CONV = 0
C_Q = C_CONV + 2 * D_CONV
C_K = C_Q + D_ATTN
C_V = C_K + D_ATTN
C_SGU = C_V + D_ATTN
C_GATE = C_SGU + 2 * D_SGU

TM = 512
TQ = 1024
TK = 512
STRIP = TK
Q_STRIPS = TQ // STRIP
N_STRIPS = 2 * Q_STRIPS
QK_AHEAD = 3
HALO = 32
V_AUG = V_DIM + 16
N_COEF = 3
POS_SPLIT = 256
LOG2E = 1.0 / math.log(2.0)
VMEM_LIMIT = 56 * 1024 * 1024

F32 = jnp.float32
BF16 = jnp.bfloat16


def _in_proj_kernel(x_ref, g_ref, w_ref, pconv_ref, q_ref, k_ref, vt_ref, psgu_ref, gate_ref):
    x = x_ref[0]
    ms = jnp.mean(x * x, axis=-1, keepdims=True)
    h = ((x * lax.rsqrt(ms + EPS)) * g_ref[...]).astype(BF16)

    def proj(lo, width):
        return jnp.dot(h, w_ref[:, lo:lo + width], preferred_element_type=F32)

    pconv_ref[0] = proj(C_CONV, 2 * D_CONV)
    q_ref[0] = (proj(C_Q, D_ATTN) * (LOG2E / math.sqrt(QK_DIM))).astype(BF16)
    k_ref[0] = proj(C_K, D_ATTN).astype(BF16)
    v = proj(C_V, D_ATTN)
    for hd in range(N_HEADS):
        for c in range(TM // TK):
            blk = v[c * TK:(c + 1) * TK, hd * V_DIM:(hd + 1) * V_DIM]
            vt_ref[0, hd, c, :V_DIM, :] = blk.T.astype(BF16)
            vt_ref[0, hd, c, V_DIM:, :] = jnp.ones((V_AUG - V_DIM, TK), BF16)
    psgu_ref[0] = proj(C_SGU, 2 * D_SGU)
    gate_ref[0] = proj(C_GATE, D_MIX)


def _in_proj(x, g, w):
    B, S, D = x.shape
    n_k = S // TK
    grid = (B, S // TM)
    row = lambda width: pl.BlockSpec((1, TM, width), lambda b, i: (b, i, 0))
    return pl.pallas_call(
        _in_proj_kernel,
        grid=grid,
        in_specs=[
            row(D),
            pl.BlockSpec((1, D), lambda b, i: (0, 0)),
            pl.BlockSpec((D, D_IN), lambda b, i: (0, 0)),
        ],
        out_specs=[
            row(2 * D_CONV),
            row(D_ATTN),
            row(D_ATTN),
            pl.BlockSpec((1, N_HEADS, TM // TK, V_AUG, TK), lambda b, i: (b, 0, i, 0, 0)),
            row(2 * D_SGU),
            row(D_MIX),
        ],
        out_shape=[
            jax.ShapeDtypeStruct((B, S, 2 * D_CONV), F32),
            jax.ShapeDtypeStruct((B, S, D_ATTN), BF16),
            jax.ShapeDtypeStruct((B, S, D_ATTN), BF16),
            jax.ShapeDtypeStruct((B, N_HEADS, n_k, V_AUG, TK), BF16),
            jax.ShapeDtypeStruct((B, S, 2 * D_SGU), F32),
            jax.ShapeDtypeStruct((B, S, D_MIX), F32),
        ],
        compiler_params=pltpu.CompilerParams(
            dimension_semantics=("arbitrary", "arbitrary"),
            vmem_limit_bytes=VMEM_LIMIT),
        name="in_proj",
    )(x, g, w)


def _attn_kernel(lam_init, q_ref, k_ref, vt_ref, pos_ref, slope_ref, lq1_ref, lk1_ref, lq2_ref,
                 lk2_ref, g_ref, o_ref, qt_ref, m_ref, acc_ref):
    i = pl.program_id(2)
    coef = slope_ref[0][:, :1] * LOG2E

    qf = q_ref[0].astype(F32).T
    sub = lax.broadcasted_iota(jnp.int32, (V_DIM, TQ), 0)
    qt_ref[:V_DIM, :TQ] = jnp.where(sub < QK_DIM, qf, 0.0).astype(BF16)
    qt_ref[:V_DIM, TQ:] = jnp.where(sub >= QK_DIM, qf, 0.0).astype(BF16)

    c0 = coef.astype(BF16).astype(F32)
    c1 = (coef - c0).astype(BF16).astype(F32)
    c2 = (coef - c0 - c1).astype(BF16).astype(F32)
    row = lax.broadcasted_iota(jnp.int32, (V_DIM, 2 * TQ), 0)
    term = row % N_COEF
    terms = jnp.where(term == 0, c0, jnp.where(term == 1, c1, c2))
    qt_ref[V_DIM:, :] = jnp.where(row < 2 * N_COEF, terms, 0.0).astype(BF16)

    m_ref[...] = jnp.full(m_ref.shape, NEG, F32)
    acc_ref[...] = jnp.zeros(acc_ref.shape, F32)

    cols = lambda s: slice(s * STRIP, (s + 1) * STRIP)

    def run(tiles):
        units = []
        for j, modes in tiles:
            kb = k_ref[0, pl.ds(pl.multiple_of(j * TK, TK), TK), :]
            kaug = jnp.concatenate([kb, pos_ref[...]], axis=1)
            c = coef * (j * TK - i * TQ).astype(F32)
            units += [(j, kaug, c, s, modes[s]) for s in range(N_STRIPS) if modes[s]]
        scores = {}

        def qk(u):
            _, kaug, _, s, _ = units[u]
            scores[u] = jnp.dot(kaug, qt_ref[:, cols(s)], preferred_element_type=F32)

        def softmax_pv(u):
            j, _, c, s, mode = units[u]
            st = scores.pop(u)
            if mode == "diag":
                kr = lax.broadcasted_iota(jnp.int32, (TK, STRIP), 0)
                qr = lax.broadcasted_iota(jnp.int32, (TK, STRIP), 1)
                st = jnp.where(kr <= qr, st, NEG)
            m_old = m_ref[:, cols(s)]
            m_new = jnp.maximum(m_old, jnp.max(st, axis=0, keepdims=True) + c)
            alpha = jnp.exp2(m_old - m_new)
            p = jnp.exp2(st - (m_new - c)).astype(BF16)
            pv = jnp.dot(vt_ref[0, 0, j], p, preferred_element_type=F32)
            acc_ref[:, cols(s)] = alpha * acc_ref[:, cols(s)] + pv
            m_ref[:, cols(s)] = m_new

        for t in range(-QK_AHEAD, len(units)):
            if t + QK_AHEAD < len(units):
                qk(t + QK_AHEAD)
            if t >= 0:
                softmax_pv(t)

    def body(jj, carry):
        run([(jj * Q_STRIPS + d, ["full"] * N_STRIPS) for d in range(Q_STRIPS)])
        return carry

    lax.fori_loop(0, i, body, 0)
    run([(i * Q_STRIPS + d,
          [None if s % Q_STRIPS < d else "diag" if s % Q_STRIPS == d else "full"
           for s in range(N_STRIPS)]) for d in range(Q_STRIPS)])

    o = acc_ref[:V_DIM, :] / acc_ref[V_DIM:V_DIM + 1, :]
    lam = (jnp.exp(jnp.sum(lq1_ref[...] * lk1_ref[...], axis=-1, keepdims=True))
           - jnp.exp(jnp.sum(lq2_ref[...] * lk2_ref[...], axis=-1, keepdims=True))
           + lam_init)
    od = o[:, :TQ] - lam * o[:, TQ:]
    ms = jnp.mean(od * od, axis=0, keepdims=True)
    y = (od * lax.rsqrt(ms + EPS)).T
    o_ref[0] = (y * g_ref[...]) * (1.0 - lam_init)


def _key_positions():
    jr = jnp.arange(TK, dtype=jnp.int32)
    lo = (jr % POS_SPLIT).astype(F32)
    hi = (jr - jr % POS_SPLIT).astype(F32)
    pos = jnp.stack([lo] * N_COEF + [hi] * N_COEF, axis=1)
    return jnp.pad(pos, ((0, 0), (0, V_DIM - 2 * N_COEF))).astype(BF16)


def _attention(q, k, vt, pos, slopes, lq1, lk1, lq2, lk2, subln_g, lam_init):
    B, S, _ = q.shape
    n_k = S // TK
    grid = (B, N_HEADS, S // TQ)
    small = lambda width: pl.BlockSpec((1, width), lambda b, h, i: (0, 0))
    return pl.pallas_call(
        functools.partial(_attn_kernel, lam_init),
        grid=grid,
        in_specs=[
            pl.BlockSpec((1, TQ, V_DIM), lambda b, h, i: (b, i, h)),
            pl.BlockSpec((1, S, V_DIM), lambda b, h, i: (b, 0, h)),
            pl.BlockSpec((1, 1, n_k, V_AUG, TK), lambda b, h, i: (b, h, 0, 0, 0)),
            pl.BlockSpec((TK, V_DIM), lambda b, h, i: (0, 0)),
            pl.BlockSpec((1, 1, 128), lambda b, h, i: (h, 0, 0)),
            small(QK_DIM), small(QK_DIM), small(QK_DIM), small(QK_DIM),
            small(V_DIM),
        ],
        out_specs=pl.BlockSpec((1, TQ, V_DIM), lambda b, h, i: (b, i, h)),
        out_shape=jax.ShapeDtypeStruct((B, S, D_ATTN), F32),
        scratch_shapes=[
            pltpu.VMEM((2 * V_DIM, 2 * TQ), BF16),
            pltpu.VMEM((1, 2 * TQ), F32),
            pltpu.VMEM((V_AUG, 2 * TQ), F32),
        ],
        compiler_params=pltpu.CompilerParams(
            dimension_semantics=("arbitrary", "arbitrary", "arbitrary"),
            vmem_limit_bytes=VMEM_LIMIT),
        name="diff_attn",
    )(q, k, vt, pos, slopes, lq1, lk1, lq2, lk2, subln_g)


def _layer_norm(v, g, b):
    mu = jnp.mean(v, axis=-1, keepdims=True)
    vc = v - mu
    var = jnp.mean(vc * vc, axis=-1, keepdims=True)
    return (vc * lax.rsqrt(var + EPS)) * g + b


def _silu(v):
    return v * jax.nn.sigmoid(v)


def _mix_kernel(final, x_ref, pconv_ref, halo_ref, yb_ref, psgu_ref, gate_ref,
                cw_ref, cb_ref, clg_ref, clb_ref, slg_ref, slb_ref, ws_ref, bs_ref,
                wout_ref, nf_ref, o_ref, zbuf_ref, ybuf_ref):
    i = pl.program_id(1)

    pc = pconv_ref[0]
    zbuf_ref[HALO:, :] = pc[:, :D_CONV] * jax.nn.sigmoid(pc[:, D_CONV:])
    ph = halo_ref[0]
    zh = ph[:, :D_CONV] * jax.nn.sigmoid(ph[:, D_CONV:])
    zbuf_ref[:HALO, :] = jnp.where(i > 0, zh, 0.0)
    conv = jnp.zeros((TM, D_CONV), F32) + cb_ref[...]
    base = HALO - (CONV_WIDTH - 1)
    for w in range(CONV_WIDTH):
        conv = conv + zbuf_ref[pl.ds(base + w, TM), :] * cw_ref[w:w + 1, :]
    y_a = _silu(_layer_norm(conv, clg_ref[...], clb_ref[...]))
    gate = gate_ref[0]
    ybuf_ref[:, :D_CONV] = (y_a * _silu(gate[:, :D_CONV])).astype(BF16)

    ybuf_ref[:, D_CONV:D_CONV + D_ATTN] = (
        yb_ref[0] * _silu(gate[:, D_CONV:D_CONV + D_ATTN])).astype(BF16)

    ps = psgu_ref[0]
    u = ps[:, :D_SGU]
    vv = _layer_norm(ps[:, D_SGU:], slg_ref[...], slb_ref[...]).astype(BF16)
    tr = lax.broadcasted_iota(jnp.int32, (CHUNK, CHUNK), 0)
    tc = lax.broadcasted_iota(jnp.int32, (CHUNK, CHUNK), 1)
    wm = [jnp.where(tr >= tc, ws_ref[g], 0.0).astype(BF16) for g in range(N_GROUPS)]
    lane_group = lax.broadcasted_iota(jnp.int32, (CHUNK, D_SGU), 1) // GROUP_DIM
    gate_c = _silu(gate[:, D_CONV + D_ATTN:])
    for c in range(TM // CHUNK):
        rows = slice(c * CHUNK, (c + 1) * CHUNK)
        vc = vv[rows, :]
        mixed = jnp.zeros((CHUNK, D_SGU), F32)
        for g in range(N_GROUPS):
            r = jnp.dot(wm[g], vc, preferred_element_type=F32)
            mixed = jnp.where(lane_group == g, r, mixed)
        y_c = u[rows, :] * (mixed + bs_ref[...])
        ybuf_ref[rows, D_CONV + D_ATTN:] = (y_c * gate_c[rows, :]).astype(BF16)

    out = x_ref[0] + jnp.dot(ybuf_ref[...], wout_ref[...], preferred_element_type=F32)
    if final:
        ms = jnp.mean(out * out, axis=-1, keepdims=True)
        out = (out * lax.rsqrt(ms + EPS)) * nf_ref[...]
    o_ref[0] = out


def _mix(final, x, pconv, yb, psgu, gate, cw, cb, clg, clb, slg, slb, ws, bs_exp, wout, nf):
    B, S, D = x.shape
    grid = (B, S // TM)
    row = lambda width: pl.BlockSpec((1, TM, width), lambda b, i: (b, i, 0))
    vec = lambda width: pl.BlockSpec((1, width), lambda b, i: (0, 0))
    halo_blocks = TM // HALO
    return pl.pallas_call(
        functools.partial(_mix_kernel, final),
        grid=grid,
        in_specs=[
            row(D),
            row(2 * D_CONV),
            pl.BlockSpec((1, HALO, 2 * D_CONV),
                         lambda b, i: (b, jnp.maximum(i * halo_blocks - 1, 0), 0)),
            row(D_ATTN),
            row(2 * D_SGU),
            row(D_MIX),
            pl.BlockSpec((CONV_WIDTH, D_CONV), lambda b, i: (0, 0)),
            vec(D_CONV), vec(D_CONV), vec(D_CONV),
            vec(D_SGU), vec(D_SGU),
            pl.BlockSpec((N_GROUPS, CHUNK, CHUNK), lambda b, i: (0, 0, 0)),
            pl.BlockSpec((CHUNK, D_SGU), lambda b, i: (0, 0)),
            pl.BlockSpec((D_MIX, D), lambda b, i: (0, 0)),
            vec(D),
        ],
        out_specs=row(D),
        out_shape=jax.ShapeDtypeStruct((B, S, D), F32),
        scratch_shapes=[
            pltpu.VMEM((TM + HALO, D_CONV), F32),
            pltpu.VMEM((TM, D_MIX), BF16),
        ],
        compiler_params=pltpu.CompilerParams(
            dimension_semantics=("arbitrary", "arbitrary"),
            vmem_limit_bytes=VMEM_LIMIT),
        name="mix_final" if final else "mix",
    )(x, pconv, pconv, yb, psgu, gate, cw, cb, clg, clb, slg, slb, ws, bs_exp, wout, nf)


def kernel(x, norm_g, w_in, conv_w, conv_b, conv_ln_g, conv_ln_b, lam_q1, lam_k1, lam_q2, lam_k2,
           subln_g, sgu_ln_g, sgu_ln_b, w_s, b_s, w_out, norm_f):
    depth = w_in.shape[0]
    slopes = jnp.exp2(-8.0 * (jnp.arange(N_HEADS, dtype=F32) + 1.0) / N_HEADS)
    slopes = jnp.broadcast_to(slopes[:, None, None], (N_HEADS, 1, 128))
    pos = _key_positions()
    r2 = lambda a: a.reshape(1, -1)
    for l in range(depth):
        lam_init = 0.8 - 0.6 * math.exp(-0.3 * l)
        pconv, q, k, vt, psgu, gate = _in_proj(x, r2(norm_g[l]), w_in[l].astype(BF16))
        yb = _attention(q, k, vt, pos, slopes, r2(lam_q1[l]), r2(lam_k1[l]), r2(lam_q2[l]),
                        r2(lam_k2[l]), r2(subln_g[l]), lam_init)
        bs_exp = jnp.repeat(b_s[l].T, GROUP_DIM, axis=1)
        x = _mix(l == depth - 1, x, pconv, yb, psgu, gate, conv_w[l], r2(conv_b[l]),
                 r2(conv_ln_g[l]), r2(conv_ln_b[l]), r2(sgu_ln_g[l]), r2(sgu_ln_b[l]),
                 w_s[l], bs_exp, w_out[l].astype(BF16), r2(norm_f))
    return x
```

```python
import functools
import math

import jax
import jax.numpy as jnp
from jax import lax
from jax.experimental import pallas as pl
from jax.experimental.pallas import tpu as pltpu

D_MODEL = 1024
D_MIX = D_MODEL
D_CONV = D_MIX // 4
D_ATTN = D_MIX // 2
D_SGU = D_MIX // 4
N_HEADS = 4
V_DIM = D_ATTN // N_HEADS
QK_DIM = V_DIM // 2
CONV_WIDTH = 31
N_GROUPS = 4
GROUP_DIM = D_SGU // N_GROUPS
CHUNK = 128
D_IN = 2 * D_CONV + 3 * D_ATTN + 2 * D_SGU + D_MIX
EPS = 1e-6
NEG = -1e30

C_CONV = 0
C_Q = C_CONV + 2 * D_CONV
C_K = C_Q + D_ATTN
C_V = C_K + D_ATTN
C_SGU = C_V + D_ATTN
C_GATE = C_SGU + 2 * D_SGU

TM = 512
TQ = 1024
TK = 512
STRIP = TK
Q_STRIPS = TQ // STRIP
N_STRIPS = 2 * Q_STRIPS
QK_AHEAD = 3
PV_BEHIND = 0
SUBLANES = 8
HALO = 32
V_AUG = V_DIM + 16
N_COEF = 3
POS_SPLIT = 256
LOG2E = 1.0 / math.log(2.0)
VMEM_LIMIT = 56 * 1024 * 1024

F32 = jnp.float32
BF16 = jnp.bfloat16


def _in_proj_kernel(x_ref, g_ref, w_ref, pconv_ref, q_ref, k_ref, vt_ref, psgu_ref, gate_ref):
    x = x_ref[0]
    ms = jnp.mean(x * x, axis=-1, keepdims=True)
    h = ((x * lax.rsqrt(ms + EPS)) * g_ref[...]).astype(BF16)

    def proj(lo, width):
        return jnp.dot(h, w_ref[:, lo:lo + width], preferred_element_type=F32)

    pconv_ref[0] = proj(C_CONV, 2 * D_CONV)
    q_ref[0] = (proj(C_Q, D_ATTN) * (LOG2E / math.sqrt(QK_DIM))).astype(BF16)
    k_ref[0] = proj(C_K, D_ATTN).astype(BF16)
    v = proj(C_V, D_ATTN)
    for hd in range(N_HEADS):
        for c in range(TM // TK):
            blk = v[c * TK:(c + 1) * TK, hd * V_DIM:(hd + 1) * V_DIM]
            vt_ref[0, hd, c, :V_DIM, :] = blk.T.astype(BF16)
            vt_ref[0, hd, c, V_DIM:, :] = jnp.ones((V_AUG - V_DIM, TK), BF16)
    psgu_ref[0] = proj(C_SGU, 2 * D_SGU)
    gate_ref[0] = proj(C_GATE, D_MIX)


def _in_proj(x, g, w):
    B, S, D = x.shape
    n_k = S // TK
    grid = (B, S // TM)
    row = lambda width: pl.BlockSpec((1, TM, width), lambda b, i: (b, i, 0))
    return pl.pallas_call(
        _in_proj_kernel,
        grid=grid,
        in_specs=[
            row(D),
            pl.BlockSpec((1, D), lambda b, i: (0, 0)),
            pl.BlockSpec((D, D_IN), lambda b, i: (0, 0)),
        ],
        out_specs=[
            row(2 * D_CONV),
            row(D_ATTN),
            row(D_ATTN),
            pl.BlockSpec((1, N_HEADS, TM // TK, V_AUG, TK), lambda b, i: (b, 0, i, 0, 0)),
            row(2 * D_SGU),
            row(D_MIX),
        ],
        out_shape=[
            jax.ShapeDtypeStruct((B, S, 2 * D_CONV), F32),
            jax.ShapeDtypeStruct((B, S, D_ATTN), BF16),
            jax.ShapeDtypeStruct((B, S, D_ATTN), BF16),
            jax.ShapeDtypeStruct((B, N_HEADS, n_k, V_AUG, TK), BF16),
            jax.ShapeDtypeStruct((B, S, 2 * D_SGU), F32),
            jax.ShapeDtypeStruct((B, S, D_MIX), F32),
        ],
        compiler_params=pltpu.CompilerParams(
            dimension_semantics=("arbitrary", "arbitrary"),
            vmem_limit_bytes=VMEM_LIMIT),
        name="in_proj",
    )(x, g, w)


def _attn_kernel(lam_init, q_ref, k_ref, vt_ref, pos_ref, slope_ref, lq1_ref, lk1_ref, lq2_ref,
                 lk2_ref, g_ref, o_ref, qt_ref, m_ref, acc_ref):
    i = pl.program_id(2)
    coef = slope_ref[0][:, :1] * LOG2E

    qf = q_ref[0].astype(F32).T
    sub = lax.broadcasted_iota(jnp.int32, (V_DIM, TQ), 0)
    qt_ref[:V_DIM, :TQ] = jnp.where(sub < QK_DIM, qf, 0.0).astype(BF16)
    qt_ref[:V_DIM, TQ:] = jnp.where(sub >= QK_DIM, qf, 0.0).astype(BF16)

    c0 = coef.astype(BF16).astype(F32)
    c1 = (coef - c0).astype(BF16).astype(F32)
    c2 = (coef - c0 - c1).astype(BF16).astype(F32)
    row = lax.broadcasted_iota(jnp.int32, (V_DIM, 2 * TQ), 0)
    term = row % N_COEF
    terms = jnp.where(term == 0, c0, jnp.where(term == 1, c1, c2))
    qt_ref[V_DIM:, :] = jnp.where(row < 2 * N_COEF, terms, 0.0).astype(BF16)

    m_ref[...] = jnp.full(m_ref.shape, NEG, F32)
    acc_ref[...] = jnp.zeros(acc_ref.shape, F32)

    cols = lambda s: slice(s * STRIP, (s + 1) * STRIP)

    def key_tile(j):
        kb = k_ref[0, pl.ds(pl.multiple_of(j * TK, TK), TK), :]
        return jnp.concatenate([kb, pos_ref[...]], axis=1)

    def score(kaug, s):
        return jnp.dot(kaug, qt_ref[:, cols(s)], preferred_element_type=F32)

    def run(tiles):
        units = []
        for j, modes in tiles:
            kaug = key_tile(j)
            c = coef * (j * TK - i * TQ).astype(F32)
            units += [(j, kaug, c, s, modes[s]) for s in range(N_STRIPS) if modes[s]]
        scores = {}

        def qk(u):
            scores[u] = score(units[u][1], units[u][3])

        probs = {}

        def softmax(u):
            _, _, c, s, mode = units[u]
            st = scores.pop(u)
            if mode == "diag":
                kr = lax.broadcasted_iota(jnp.int32, (TK, STRIP), 0)
                qr = lax.broadcasted_iota(jnp.int32, (TK, STRIP), 1)
                st = jnp.where(kr <= qr, st, NEG)
            m_old = m_ref[:, cols(s)]
            m_new = jnp.maximum(m_old, jnp.max(st, axis=0, keepdims=True) + c)
            alpha = jnp.exp2(m_old - m_new)
            probs[u] = (jnp.exp2(st - (m_new - c)).astype(BF16), alpha)
            m_ref[:, cols(s)] = m_new

        def pv(u):
            j, _, _, s, _ = units[u]
            p, alpha = probs.pop(u)
            o = jnp.dot(vt_ref[0, 0, j], p, preferred_element_type=F32)
            acc_ref[:, cols(s)] = alpha * acc_ref[:, cols(s)] + o

        for t in range(-QK_AHEAD, len(units) + PV_BEHIND):
            if t + QK_AHEAD < len(units):
                qk(t + QK_AHEAD)
            if 0 <= t < len(units):
                softmax(t)
            if 0 <= t - PV_BEHIND:
                pv(t - PV_BEHIND)

    def body(jj, carry):
        run([(jj * Q_STRIPS + d, ["full"] * N_STRIPS) for d in range(Q_STRIPS)])
        return carry

    lax.fori_loop(0, i, body, 0)
    run([(i * Q_STRIPS + d,
          [None if s % Q_STRIPS < d else "diag" if s % Q_STRIPS == d else "full"
           for s in range(N_STRIPS)]) for d in range(Q_STRIPS)])

    o = acc_ref[:V_DIM, :] / acc_ref[V_DIM:V_DIM + 1, :]
    lam = (jnp.exp(jnp.sum(lq1_ref[...] * lk1_ref[...], axis=-1, keepdims=True))
           - jnp.exp(jnp.sum(lq2_ref[...] * lk2_ref[...], axis=-1, keepdims=True))
           + lam_init)
    od = o[:, :TQ] - lam * o[:, TQ:]
    ms = jnp.mean(od * od, axis=0, keepdims=True)
    y = (od * lax.rsqrt(ms + EPS)).T
    o_ref[0] = (y * g_ref[...]) * (1.0 - lam_init)


def _key_positions():
    jr = jnp.arange(TK, dtype=jnp.int32)
    lo = (jr % POS_SPLIT).astype(F32)
    hi = (jr - jr % POS_SPLIT).astype(F32)
    pos = jnp.stack([lo] * N_COEF + [hi] * N_COEF, axis=1)
    return jnp.pad(pos, ((0, 0), (0, V_DIM - 2 * N_COEF))).astype(BF16)


def _attention(q, k, vt, pos, slopes, lq1, lk1, lq2, lk2, subln_g, lam_init):
    B, S, _ = q.shape
    n_k = S // TK
    grid = (B, N_HEADS, S // TQ)
    small = lambda width: pl.BlockSpec((1, width), lambda b, h, i: (0, 0))
    return pl.pallas_call(
        functools.partial(_attn_kernel, lam_init),
        grid=grid,
        in_specs=[
            pl.BlockSpec((1, TQ, V_DIM), lambda b, h, i: (b, i, h)),
            pl.BlockSpec((1, S, V_DIM), lambda b, h, i: (b, 0, h)),
            pl.BlockSpec((1, 1, n_k, V_AUG, TK), lambda b, h, i: (b, h, 0, 0, 0)),
            pl.BlockSpec((TK, V_DIM), lambda b, h, i: (0, 0)),
            pl.BlockSpec((1, 1, 128), lambda b, h, i: (h, 0, 0)),
            small(QK_DIM), small(QK_DIM), small(QK_DIM), small(QK_DIM),
            small(V_DIM),
        ],
        out_specs=pl.BlockSpec((1, TQ, V_DIM), lambda b, h, i: (b, i, h)),
        out_shape=jax.ShapeDtypeStruct((B, S, D_ATTN), F32),
        scratch_shapes=[
            pltpu.VMEM((2 * V_DIM, 2 * TQ), BF16),
            pltpu.VMEM((1, 2 * TQ), F32),
            pltpu.VMEM((V_AUG, 2 * TQ), F32),
        ],
        compiler_params=pltpu.CompilerParams(
            dimension_semantics=("arbitrary", "arbitrary", "arbitrary"),
            vmem_limit_bytes=VMEM_LIMIT),
        name="diff_attn",
    )(q, k, vt, pos, slopes, lq1, lk1, lq2, lk2, subln_g)


def _layer_norm(v, g, b):
    mu = jnp.mean(v, axis=-1, keepdims=True)
    vc = v - mu
    var = jnp.mean(vc * vc, axis=-1, keepdims=True)
    return (vc * lax.rsqrt(var + EPS)) * g + b


def _silu(v):
    return v * jax.nn.sigmoid(v)


def _mix_kernel(final, x_ref, pconv_ref, halo_ref, yb_ref, psgu_ref, gate_ref,
                cw_ref, cb_ref, clg_ref, clb_ref, slg_ref, slb_ref, ws_ref, bs_ref,
                wout_ref, nf_ref, o_ref, zbuf_ref, ybuf_ref):
    i = pl.program_id(1)

    pc = pconv_ref[0]
    zbuf_ref[HALO:, :] = pc[:, :D_CONV] * jax.nn.sigmoid(pc[:, D_CONV:])
    ph = halo_ref[0]
    zh = ph[:, :D_CONV] * jax.nn.sigmoid(ph[:, D_CONV:])
    zbuf_ref[:HALO, :] = jnp.where(i > 0, zh, 0.0)
    base = HALO - (CONV_WIDTH - 1)
    conv = jnp.zeros((TM, D_CONV), F32) + cb_ref[...]
    for b in range(SUBLANES):
        rows = TM if b == 0 else TM + SUBLANES
        part = None
        for w in range(CONV_WIDTH):
            if (base + w) % SUBLANES != b:
                continue
            term = zbuf_ref[pl.ds(base + w - b, rows), :] * cw_ref[w:w + 1, :]
            part = term if part is None else part + term
        conv = conv + (part if b == 0 else part[b:b + TM, :])
    y_a = _silu(_layer_norm(conv, clg_ref[...], clb_ref[...]))
    gate = gate_ref[0]
    ybuf_ref[:, :D_CONV] = (y_a * _silu(gate[:, :D_CONV])).astype(BF16)

    ybuf_ref[:, D_CONV:D_CONV + D_ATTN] = (
        yb_ref[0] * _silu(gate[:, D_CONV:D_CONV + D_ATTN])).astype(BF16)

    ps = psgu_ref[0]
    u = ps[:, :D_SGU]
    vv = _layer_norm(ps[:, D_SGU:], slg_ref[...], slb_ref[...]).astype(BF16)
    tr = lax.broadcasted_iota(jnp.int32, (CHUNK, CHUNK), 0)
    tc = lax.broadcasted_iota(jnp.int32, (CHUNK, CHUNK), 1)
    wm = [jnp.where(tr >= tc, ws_ref[g], 0.0).astype(BF16) for g in range(N_GROUPS)]
    lane_group = lax.broadcasted_iota(jnp.int32, (CHUNK, D_SGU), 1) // GROUP_DIM
    gate_c = _silu(gate[:, D_CONV + D_ATTN:])
    for c in range(TM // CHUNK):
        rows = slice(c * CHUNK, (c + 1) * CHUNK)
        vc = vv[rows, :]
        mixed = jnp.zeros((CHUNK, D_SGU), F32)
        for g in range(N_GROUPS):
            r = jnp.dot(wm[g], vc, preferred_element_type=F32)
            mixed = jnp.where(lane_group == g, r, mixed)
        y_c = u[rows, :] * (mixed + bs_ref[...])
        ybuf_ref[rows, D_CONV + D_ATTN:] = (y_c * gate_c[rows, :]).astype(BF16)

    out = x_ref[0] + jnp.dot(ybuf_ref[...], wout_ref[...], preferred_element_type=F32)
    if final:
        ms = jnp.mean(out * out, axis=-1, keepdims=True)
        out = (out * lax.rsqrt(ms + EPS)) * nf_ref[...]
    o_ref[0] = out


def _mix(final, x, pconv, yb, psgu, gate, cw, cb, clg, clb, slg, slb, ws, bs_exp, wout, nf):
    B, S, D = x.shape
    grid = (B, S // TM)
    row = lambda width: pl.BlockSpec((1, TM, width), lambda b, i: (b, i, 0))
    vec = lambda width: pl.BlockSpec((1, width), lambda b, i: (0, 0))
    halo_blocks = TM // HALO
    return pl.pallas_call(
        functools.partial(_mix_kernel, final),
        grid=grid,
        in_specs=[
            row(D),
            row(2 * D_CONV),
            pl.BlockSpec((1, HALO, 2 * D_CONV),
                         lambda b, i: (b, jnp.maximum(i * halo_blocks - 1, 0), 0)),
            row(D_ATTN),
            row(2 * D_SGU),
            row(D_MIX),
            pl.BlockSpec((CONV_WIDTH, D_CONV), lambda b, i: (0, 0)),
            vec(D_CONV), vec(D_CONV), vec(D_CONV),
            vec(D_SGU), vec(D_SGU),
            pl.BlockSpec((N_GROUPS, CHUNK, CHUNK), lambda b, i: (0, 0, 0)),
            pl.BlockSpec((CHUNK, D_SGU), lambda b, i: (0, 0)),
            pl.BlockSpec((D_MIX, D), lambda b, i: (0, 0)),
            vec(D),
        ],
        out_specs=row(D),
        out_shape=jax.ShapeDtypeStruct((B, S, D), F32),
        scratch_shapes=[
            pltpu.VMEM((TM + HALO, D_CONV), F32),
            pltpu.VMEM((TM, D_MIX), BF16),
        ],
        compiler_params=pltpu.CompilerParams(
            dimension_semantics=("arbitrary", "arbitrary"),
            vmem_limit_bytes=VMEM_LIMIT),
        name="mix_final" if final else "mix",
    )(x, pconv, pconv, yb, psgu, gate, cw, cb, clg, clb, slg, slb, ws, bs_exp, wout, nf)


def kernel(x, norm_g, w_in, conv_w, conv_b, conv_ln_g, conv_ln_b, lam_q1, lam_k1, lam_q2, lam_k2,
           subln_g, sgu_ln_g, sgu_ln_b, w_s, b_s, w_out, norm_f):
    depth = w_in.shape[0]
    slopes = jnp.exp2(-8.0 * (jnp.arange(N_HEADS, dtype=F32) + 1.0) / N_HEADS)
    slopes = jnp.broadcast_to(slopes[:, None, None], (N_HEADS, 1, 128))
    pos = _key_positions()
    r2 = lambda a: a.reshape(1, -1)
    for l in range(depth):
        lam_init = 0.8 - 0.6 * math.exp(-0.3 * l)
        pconv, q, k, vt, psgu, gate = _in_proj(x, r2(norm_g[l]), w_in[l].astype(BF16))
        yb = _attention(q, k, vt, pos, slopes, r2(lam_q1[l]), r2(lam_k1[l]), r2(lam_q2[l]),
                        r2(lam_k2[l]), r2(subln_g[l]), lam_init)
        bs_exp = jnp.repeat(b_s[l].T, GROUP_DIM, axis=1)
        x = _mix(l == depth - 1, x, pconv, yb, psgu, gate, conv_w[l], r2(conv_b[l]),
                 r2(conv_ln_g[l]), r2(conv_ln_b[l]), r2(sgu_ln_g[l]), r2(sgu_ln_b[l]),
                 w_s[l], bs_exp, w_out[l].astype(BF16), r2(norm_f))
    return x
```

```python
import functools
import math

import jax
import jax.numpy as jnp
from jax import lax
from jax.experimental import pallas as pl
from jax.experimental.pallas import tpu as pltpu

D_MODEL = 1024
D_MIX = D_MODEL
D_CONV = D_MIX // 4
D_ATTN = D_MIX // 2
D_SGU = D_MIX // 4
N_HEADS = 4
V_DIM = D_ATTN // N_HEADS
QK_DIM = V_DIM // 2
CONV_WIDTH = 31
N_GROUPS = 4
GROUP_DIM = D_SGU // N_GROUPS
CHUNK = 128
D_IN = 2 * D_CONV + 3 * D_ATTN + 2 * D_SGU + D_MIX
EPS = 1e-6
NEG = -1e30

C_CONV = 0
C_Q = C_CONV + 2 * D_CONV
C_K = C_Q + D_ATTN
C_V = C_K + D_ATTN
C_SGU = C_V + D_ATTN
C_GATE = C_SGU + 2 * D_SGU

TM = 512
TQ = 2048
TK = 512
STRIP = TK
Q_STRIPS = TQ // STRIP
N_STRIPS = 2 * Q_STRIPS
QK_AHEAD = 3
PV_BEHIND = 0
SUBLANES = 8
HALO = 32
V_AUG = V_DIM + 16
N_COEF = 3
POS_SPLIT = 256
LOG2E = 1.0 / math.log(2.0)
VMEM_LIMIT = 56 * 1024 * 1024

F32 = jnp.float32
BF16 = jnp.bfloat16


def _in_proj_kernel(x_ref, g_ref, w_ref, pconv_ref, q_ref, k_ref, vt_ref, psgu_ref, gate_ref):
    x = x_ref[0]
    ms = jnp.mean(x * x, axis=-1, keepdims=True)
    h = ((x * lax.rsqrt(ms + EPS)) * g_ref[...]).astype(BF16)

    def proj(lo, width):
        return jnp.dot(h, w_ref[:, lo:lo + width], preferred_element_type=F32)

    pconv_ref[0] = proj(C_CONV, 2 * D_CONV)
    q_ref[0] = (proj(C_Q, D_ATTN) * (LOG2E / math.sqrt(QK_DIM))).astype(BF16)
    k_ref[0] = proj(C_K, D_ATTN).astype(BF16)
    v = proj(C_V, D_ATTN)
    for hd in range(N_HEADS):
        for c in range(TM // TK):
            blk = v[c * TK:(c + 1) * TK, hd * V_DIM:(hd + 1) * V_DIM]
            vt_ref[0, hd, c, :V_DIM, :] = blk.T.astype(BF16)
            vt_ref[0, hd, c, V_DIM:, :] = jnp.ones((V_AUG - V_DIM, TK), BF16)
    psgu_ref[0] = proj(C_SGU, 2 * D_SGU)
    gate_ref[0] = proj(C_GATE, D_MIX)


def _in_proj(x, g, w):
    B, S, D = x.shape
    n_k = S // TK
    grid = (B, S // TM)
    row = lambda width: pl.BlockSpec((1, TM, width), lambda b, i: (b, i, 0))
    return pl.pallas_call(
        _in_proj_kernel,
        grid=grid,
        in_specs=[
            row(D),
            pl.BlockSpec((1, D), lambda b, i: (0, 0)),
            pl.BlockSpec((D, D_IN), lambda b, i: (0, 0)),
        ],
        out_specs=[
            row(2 * D_CONV),
            row(D_ATTN),
            row(D_ATTN),
            pl.BlockSpec((1, N_HEADS, TM // TK, V_AUG, TK), lambda b, i: (b, 0, i, 0, 0)),
            row(2 * D_SGU),
            row(D_MIX),
        ],
        out_shape=[
            jax.ShapeDtypeStruct((B, S, 2 * D_CONV), F32),
            jax.ShapeDtypeStruct((B, S, D_ATTN), BF16),
            jax.ShapeDtypeStruct((B, S, D_ATTN), BF16),
            jax.ShapeDtypeStruct((B, N_HEADS, n_k, V_AUG, TK), BF16),
            jax.ShapeDtypeStruct((B, S, 2 * D_SGU), F32),
            jax.ShapeDtypeStruct((B, S, D_MIX), F32),
        ],
        compiler_params=pltpu.CompilerParams(
            dimension_semantics=("arbitrary", "arbitrary"),
            vmem_limit_bytes=VMEM_LIMIT),
        name="in_proj",
    )(x, g, w)


def _attn_kernel(lam_init, q_ref, k_ref, vt_ref, pos_ref, slope_ref, lq1_ref, lk1_ref, lq2_ref,
                 lk2_ref, g_ref, o_ref, qt_ref, m_ref, acc_ref):
    i = pl.program_id(2)
    coef = slope_ref[0][:, :1] * LOG2E

    qf = q_ref[0].astype(F32).T
    sub = lax.broadcasted_iota(jnp.int32, (V_DIM, TQ), 0)
    qt_ref[:V_DIM, :TQ] = jnp.where(sub < QK_DIM, qf, 0.0).astype(BF16)
    qt_ref[:V_DIM, TQ:] = jnp.where(sub >= QK_DIM, qf, 0.0).astype(BF16)

    c0 = coef.astype(BF16).astype(F32)
    c1 = (coef - c0).astype(BF16).astype(F32)
    c2 = (coef - c0 - c1).astype(BF16).astype(F32)
    row = lax.broadcasted_iota(jnp.int32, (V_DIM, 2 * TQ), 0)
    term = row % N_COEF
    terms = jnp.where(term == 0, c0, jnp.where(term == 1, c1, c2))
    qt_ref[V_DIM:, :] = jnp.where(row < 2 * N_COEF, terms, 0.0).astype(BF16)

    m_ref[...] = jnp.full(m_ref.shape, NEG, F32)
    acc_ref[...] = jnp.zeros(acc_ref.shape, F32)

    cols = lambda s: slice(s * STRIP, (s + 1) * STRIP)

    def key_tile(j):
        kb = k_ref[0, pl.ds(pl.multiple_of(j * TK, TK), TK), :]
        return jnp.concatenate([kb, pos_ref[...]], axis=1)

    def score(kaug, s):
        return jnp.dot(kaug, qt_ref[:, cols(s)], preferred_element_type=F32)

    def run(tiles):
        units = []
        for j, modes in tiles:
            kaug = key_tile(j)
            c = coef * (j * TK - i * TQ).astype(F32)
            units += [(j, kaug, c, s, modes[s]) for s in range(N_STRIPS) if modes[s]]
        scores = {}

        def qk(u):
            scores[u] = score(units[u][1], units[u][3])

        probs = {}

        def softmax(u):
            _, _, c, s, mode = units[u]
            st = scores.pop(u)
            if mode == "diag":
                kr = lax.broadcasted_iota(jnp.int32, (TK, STRIP), 0)
                qr = lax.broadcasted_iota(jnp.int32, (TK, STRIP), 1)
                st = jnp.where(kr <= qr, st, NEG)
            m_old = m_ref[:, cols(s)]
            m_new = jnp.maximum(m_old, jnp.max(st, axis=0, keepdims=True) + c)
            alpha = jnp.exp2(m_old - m_new)
            probs[u] = (jnp.exp2(st - (m_new - c)).astype(BF16), alpha)
            m_ref[:, cols(s)] = m_new

        def pv(u):
            j, _, _, s, _ = units[u]
            p, alpha = probs.pop(u)
            o = jnp.dot(vt_ref[0, 0, j], p, preferred_element_type=F32)
            acc_ref[:, cols(s)] = alpha * acc_ref[:, cols(s)] + o

        for t in range(-QK_AHEAD, len(units) + PV_BEHIND):
            if t + QK_AHEAD < len(units):
                qk(t + QK_AHEAD)
            if 0 <= t < len(units):
                softmax(t)
            if 0 <= t - PV_BEHIND:
                pv(t - PV_BEHIND)

    def body(jj, carry):
        run([(jj * Q_STRIPS + d, ["full"] * N_STRIPS) for d in range(Q_STRIPS)])
        return carry

    lax.fori_loop(0, i, body, 0)
    run([(i * Q_STRIPS + d,
          [None if s % Q_STRIPS < d else "diag" if s % Q_STRIPS == d else "full"
           for s in range(N_STRIPS)]) for d in range(Q_STRIPS)])

    o = acc_ref[:V_DIM, :] / acc_ref[V_DIM:V_DIM + 1, :]
    lam = (jnp.exp(jnp.sum(lq1_ref[...] * lk1_ref[...], axis=-1, keepdims=True))
           - jnp.exp(jnp.sum(lq2_ref[...] * lk2_ref[...], axis=-1, keepdims=True))
           + lam_init)
    od = o[:, :TQ] - lam * o[:, TQ:]
    ms = jnp.mean(od * od, axis=0, keepdims=True)
    y = (od * lax.rsqrt(ms + EPS)).T
    o_ref[0] = (y * g_ref[...]) * (1.0 - lam_init)


def _key_positions():
    jr = jnp.arange(TK, dtype=jnp.int32)
    lo = (jr % POS_SPLIT).astype(F32)
    hi = (jr - jr % POS_SPLIT).astype(F32)
    pos = jnp.stack([lo] * N_COEF + [hi] * N_COEF, axis=1)
    return jnp.pad(pos, ((0, 0), (0, V_DIM - 2 * N_COEF))).astype(BF16)


def _attention(q, k, vt, pos, slopes, lq1, lk1, lq2, lk2, subln_g, lam_init):
    B, S, _ = q.shape
    n_k = S // TK
    grid = (B, N_HEADS, S // TQ)
    small = lambda width: pl.BlockSpec((1, width), lambda b, h, i: (0, 0))
    return pl.pallas_call(
        functools.partial(_attn_kernel, lam_init),
        grid=grid,
        in_specs=[
            pl.BlockSpec((1, TQ, V_DIM), lambda b, h, i: (b, i, h)),
            pl.BlockSpec((1, S, V_DIM), lambda b, h, i: (b, 0, h)),
            pl.BlockSpec((1, 1, n_k, V_AUG, TK), lambda b, h, i: (b, h, 0, 0, 0)),
            pl.BlockSpec((TK, V_DIM), lambda b, h, i: (0, 0)),
            pl.BlockSpec((1, 1, 128), lambda b, h, i: (h, 0, 0)),
            small(QK_DIM), small(QK_DIM), small(QK_DIM), small(QK_DIM),
            small(V_DIM),
        ],
        out_specs=pl.BlockSpec((1, TQ, V_DIM), lambda b, h, i: (b, i, h)),
        out_shape=jax.ShapeDtypeStruct((B, S, D_ATTN), F32),
        scratch_shapes=[
            pltpu.VMEM((2 * V_DIM, 2 * TQ), BF16),
            pltpu.VMEM((1, 2 * TQ), F32),
            pltpu.VMEM((V_AUG, 2 * TQ), F32),
        ],
        compiler_params=pltpu.CompilerParams(
            dimension_semantics=("arbitrary", "arbitrary", "arbitrary"),
            vmem_limit_bytes=VMEM_LIMIT),
        name="diff_attn",
    )(q, k, vt, pos, slopes, lq1, lk1, lq2, lk2, subln_g)


def _layer_norm(v, g, b):
    mu = jnp.mean(v, axis=-1, keepdims=True)
    vc = v - mu
    var = jnp.mean(vc * vc, axis=-1, keepdims=True)
    return (vc * lax.rsqrt(var + EPS)) * g + b


def _silu(v):
    return v * jax.nn.sigmoid(v)


def _mix_kernel(final, x_ref, pconv_ref, halo_ref, yb_ref, psgu_ref, gate_ref,
                cw_ref, cb_ref, clg_ref, clb_ref, slg_ref, slb_ref, ws_ref, bs_ref,
                wout_ref, nf_ref, o_ref, zbuf_ref, ybuf_ref):
    i = pl.program_id(1)

    pc = pconv_ref[0]
    zbuf_ref[HALO:, :] = pc[:, :D_CONV] * jax.nn.sigmoid(pc[:, D_CONV:])
    ph = halo_ref[0]
    zh = ph[:, :D_CONV] * jax.nn.sigmoid(ph[:, D_CONV:])
    zbuf_ref[:HALO, :] = jnp.where(i > 0, zh, 0.0)
    base = HALO - (CONV_WIDTH - 1)
    conv = jnp.zeros((TM, D_CONV), F32) + cb_ref[...]
    for b in range(SUBLANES):
        rows = TM if b == 0 else TM + SUBLANES
        part = None
        for w in range(CONV_WIDTH):
            if (base + w) % SUBLANES != b:
                continue
            term = zbuf_ref[pl.ds(base + w - b, rows), :] * cw_ref[w:w + 1, :]
            part = term if part is None else part + term
        conv = conv + (part if b == 0 else part[b:b + TM, :])
    y_a = _silu(_layer_norm(conv, clg_ref[...], clb_ref[...]))
    gate = gate_ref[0]
    ybuf_ref[:, :D_CONV] = (y_a * _silu(gate[:, :D_CONV])).astype(BF16)

    ybuf_ref[:, D_CONV:D_CONV + D_ATTN] = (
        yb_ref[0] * _silu(gate[:, D_CONV:D_CONV + D_ATTN])).astype(BF16)

    ps = psgu_ref[0]
    u = ps[:, :D_SGU]
    vv = _layer_norm(ps[:, D_SGU:], slg_ref[...], slb_ref[...]).astype(BF16)
    tr = lax.broadcasted_iota(jnp.int32, (CHUNK, CHUNK), 0)
    tc = lax.broadcasted_iota(jnp.int32, (CHUNK, CHUNK), 1)
    wm = [jnp.where(tr >= tc, ws_ref[g], 0.0).astype(BF16) for g in range(N_GROUPS)]
    lane_group = lax.broadcasted_iota(jnp.int32, (CHUNK, D_SGU), 1) // GROUP_DIM
    gate_c = _silu(gate[:, D_CONV + D_ATTN:])
    for c in range(TM // CHUNK):
        rows = slice(c * CHUNK, (c + 1) * CHUNK)
        vc = vv[rows, :]
        mixed = jnp.zeros((CHUNK, D_SGU), F32)
        for g in range(N_GROUPS):
            r = jnp.dot(wm[g], vc, preferred_element_type=F32)
            mixed = jnp.where(lane_group == g, r, mixed)
        y_c = u[rows, :] * (mixed + bs_ref[...])
        ybuf_ref[rows, D_CONV + D_ATTN:] = (y_c * gate_c[rows, :]).astype(BF16)

    out = x_ref[0] + jnp.dot(ybuf_ref[...], wout_ref[...], preferred_element_type=F32)
    if final:
        ms = jnp.mean(out * out, axis=-1, keepdims=True)
        out = (out * lax.rsqrt(ms + EPS)) * nf_ref[...]
    o_ref[0] = out


def _mix(final, x, pconv, yb, psgu, gate, cw, cb, clg, clb, slg, slb, ws, bs_exp, wout, nf):
    B, S, D = x.shape
    grid = (B, S // TM)
    row = lambda width: pl.BlockSpec((1, TM, width), lambda b, i: (b, i, 0))
    vec = lambda width: pl.BlockSpec((1, width), lambda b, i: (0, 0))
    halo_blocks = TM // HALO
    return pl.pallas_call(
        functools.partial(_mix_kernel, final),
        grid=grid,
        in_specs=[
            row(D),
            row(2 * D_CONV),
            pl.BlockSpec((1, HALO, 2 * D_CONV),
                         lambda b, i: (b, jnp.maximum(i * halo_blocks - 1, 0), 0)),
            row(D_ATTN),
            row(2 * D_SGU),
            row(D_MIX),
            pl.BlockSpec((CONV_WIDTH, D_CONV), lambda b, i: (0, 0)),
            vec(D_CONV), vec(D_CONV), vec(D_CONV),
            vec(D_SGU), vec(D_SGU),
            pl.BlockSpec((N_GROUPS, CHUNK, CHUNK), lambda b, i: (0, 0, 0)),
            pl.BlockSpec((CHUNK, D_SGU), lambda b, i: (0, 0)),
            pl.BlockSpec((D_MIX, D), lambda b, i: (0, 0)),
            vec(D),
        ],
        out_specs=row(D),
        out_shape=jax.ShapeDtypeStruct((B, S, D), F32),
        scratch_shapes=[
            pltpu.VMEM((TM + HALO, D_CONV), F32),
            pltpu.VMEM((TM, D_MIX), BF16),
        ],
        compiler_params=pltpu.CompilerParams(
            dimension_semantics=("arbitrary", "arbitrary"),
            vmem_limit_bytes=VMEM_LIMIT),
        name="mix_final" if final else "mix",
    )(x, pconv, pconv, yb, psgu, gate, cw, cb, clg, clb, slg, slb, ws, bs_exp, wout, nf)


def kernel(x, norm_g, w_in, conv_w, conv_b, conv_ln_g, conv_ln_b, lam_q1, lam_k1, lam_q2, lam_k2,
           subln_g, sgu_ln_g, sgu_ln_b, w_s, b_s, w_out, norm_f):
    depth = w_in.shape[0]
    slopes = jnp.exp2(-8.0 * (jnp.arange(N_HEADS, dtype=F32) + 1.0) / N_HEADS)
    slopes = jnp.broadcast_to(slopes[:, None, None], (N_HEADS, 1, 128))
    pos = _key_positions()
    r2 = lambda a: a.reshape(1, -1)
    for l in range(depth):
        lam_init = 0.8 - 0.6 * math.exp(-0.3 * l)
        pconv, q, k, vt, psgu, gate = _in_proj(x, r2(norm_g[l]), w_in[l].astype(BF16))
        yb = _attention(q, k, vt, pos, slopes, r2(lam_q1[l]), r2(lam_k1[l]), r2(lam_q2[l]),
                        r2(lam_k2[l]), r2(subln_g[l]), lam_init)
        bs_exp = jnp.repeat(b_s[l].T, GROUP_DIM, axis=1)
        x = _mix(l == depth - 1, x, pconv, yb, psgu, gate, conv_w[l], r2(conv_b[l]),
                 r2(conv_ln_g[l]), r2(conv_ln_b[l]), r2(sgu_ln_g[l]), r2(sgu_ln_b[l]),
                 w_s[l], bs_exp, w_out[l].astype(BF16), r2(norm_f))
    return x
```

```python
import functools
import math

import jax
import jax.numpy as jnp
from jax import lax
from jax.experimental import pallas as pl
from jax.experimental.pallas import tpu as pltpu

D_MODEL = 1024
D_MIX = D_MODEL
D_CONV = D_MIX // 4
D_ATTN = D_MIX // 2
D_SGU = D_MIX // 4
N_HEADS = 4
V_DIM = D_ATTN // N_HEADS
QK_DIM = V_DIM // 2
CONV_WIDTH = 31
N_GROUPS = 4
GROUP_DIM = D_SGU // N_GROUPS
CHUNK = 128
D_IN = 2 * D_CONV + 3 * D_ATTN + 2 * D_SGU + D_MIX
EPS = 1e-6
NEG = -1e30

C_CONV = 0
C_Q = C_CONV + 2 * D_CONV
C_K = C_Q + D_ATTN
C_V = C_K + D_ATTN
C_SGU = C_V + D_ATTN
C_GATE = C_SGU + 2 * D_SGU

TM = 512
TQ = 2048
TK = 512
STRIP = TK
Q_STRIPS = TQ // STRIP
N_STRIPS = 2 * Q_STRIPS
QK_AHEAD = 4
S_SLOTS = QK_AHEAD + 1
SUBLANES = 8
HALO = 32
V_AUG = V_DIM + 16
N_COEF = 3
POS_SPLIT = 256
LOG2E = 1.0 / math.log(2.0)
VMEM_LIMIT = 56 * 1024 * 1024

F32 = jnp.float32
BF16 = jnp.bfloat16


def _in_proj_kernel(x_ref, g_ref, w_ref, pconv_ref, q_ref, k_ref, vt_ref, psgu_ref, gate_ref):
    x = x_ref[0]
    ms = jnp.mean(x * x, axis=-1, keepdims=True)
    h = ((x * lax.rsqrt(ms + EPS)) * g_ref[...]).astype(BF16)

    def proj(lo, width):
        return jnp.dot(h, w_ref[:, lo:lo + width], preferred_element_type=F32)

    pconv_ref[0] = proj(C_CONV, 2 * D_CONV)
    q_ref[0] = (proj(C_Q, D_ATTN) * (LOG2E / math.sqrt(QK_DIM))).astype(BF16)
    k_ref[0] = proj(C_K, D_ATTN).astype(BF16)
    v = proj(C_V, D_ATTN)
    for hd in range(N_HEADS):
        for c in range(TM // TK):
            blk = v[c * TK:(c + 1) * TK, hd * V_DIM:(hd + 1) * V_DIM]
            vt_ref[0, hd, c, :V_DIM, :] = blk.T.astype(BF16)
            vt_ref[0, hd, c, V_DIM:, :] = jnp.ones((V_AUG - V_DIM, TK), BF16)
    psgu_ref[0] = proj(C_SGU, 2 * D_SGU)
    gate_ref[0] = proj(C_GATE, D_MIX)


def _in_proj(x, g, w):
    B, S, D = x.shape
    n_k = S // TK
    grid = (B, S // TM)
    row = lambda width: pl.BlockSpec((1, TM, width), lambda b, i: (b, i, 0))
    return pl.pallas_call(
        _in_proj_kernel,
        grid=grid,
        in_specs=[
            row(D),
            pl.BlockSpec((1, D), lambda b, i: (0, 0)),
            pl.BlockSpec((D, D_IN), lambda b, i: (0, 0)),
        ],
        out_specs=[
            row(2 * D_CONV),
            row(D_ATTN),
            row(D_ATTN),
            pl.BlockSpec((1, N_HEADS, TM // TK, V_AUG, TK), lambda b, i: (b, 0, i, 0, 0)),
            row(2 * D_SGU),
            row(D_MIX),
        ],
        out_shape=[
            jax.ShapeDtypeStruct((B, S, 2 * D_CONV), F32),
            jax.ShapeDtypeStruct((B, S, D_ATTN), BF16),
            jax.ShapeDtypeStruct((B, S, D_ATTN), BF16),
            jax.ShapeDtypeStruct((B, N_HEADS, n_k, V_AUG, TK), BF16),
            jax.ShapeDtypeStruct((B, S, 2 * D_SGU), F32),
            jax.ShapeDtypeStruct((B, S, D_MIX), F32),
        ],
        compiler_params=pltpu.CompilerParams(
            dimension_semantics=("arbitrary", "arbitrary"),
            vmem_limit_bytes=VMEM_LIMIT),
        name="in_proj",
    )(x, g, w)


def _attn_kernel(lam_init, q_ref, k_ref, vt_ref, pos_ref, slope_ref, lq1_ref, lk1_ref, lq2_ref,
                 lk2_ref, g_ref, o_ref, qt_ref, m_ref, acc_ref, st_ref):
    i = pl.program_id(2)
    coef = slope_ref[0][:, :1] * LOG2E

    qf = q_ref[0].astype(F32).T
    sub = lax.broadcasted_iota(jnp.int32, (V_DIM, TQ), 0)
    qt_ref[:V_DIM, :TQ] = jnp.where(sub < QK_DIM, qf, 0.0).astype(BF16)
    qt_ref[:V_DIM, TQ:] = jnp.where(sub >= QK_DIM, qf, 0.0).astype(BF16)

    c0 = coef.astype(BF16).astype(F32)
    c1 = (coef - c0).astype(BF16).astype(F32)
    c2 = (coef - c0 - c1).astype(BF16).astype(F32)
    row = lax.broadcasted_iota(jnp.int32, (V_DIM, 2 * TQ), 0)
    term = row % N_COEF
    terms = jnp.where(term == 0, c0, jnp.where(term == 1, c1, c2))
    qt_ref[V_DIM:, :] = jnp.where(row < 2 * N_COEF, terms, 0.0).astype(BF16)

    m_ref[...] = jnp.full(m_ref.shape, NEG, F32)
    acc_ref[...] = jnp.zeros(acc_ref.shape, F32)

    cols = lambda s: slice(s * STRIP, (s + 1) * STRIP)

    def key_tile(j):
        kb = k_ref[0, pl.ds(pl.multiple_of(j * TK, TK), TK), :]
        return jnp.concatenate([kb, pos_ref[...]], axis=1)

    def score(kaug, s):
        return jnp.dot(kaug, qt_ref[:, cols(s)], preferred_element_type=F32)

    def run(tiles):
        units = []
        for j, modes in tiles:
            kaug = key_tile(j)
            c = coef * (j * TK - i * TQ).astype(F32)
            units += [(j, kaug, c, s, modes[s]) for s in range(N_STRIPS) if modes[s]]

        def qk(u):
            st_ref[u % S_SLOTS] = score(units[u][1], units[u][3])

        def softmax_pv(u):
            j, _, c, s, mode = units[u]
            st = st_ref.at[u % S_SLOTS]
            if mode == "diag":
                kr = lax.broadcasted_iota(jnp.int32, (TK, STRIP), 0)
                qr = lax.broadcasted_iota(jnp.int32, (TK, STRIP), 1)
                st[...] = jnp.where(kr <= qr, st[...], NEG)
            m_old = m_ref[:, cols(s)]
            m_new = jnp.maximum(m_old, jnp.max(st[...], axis=0, keepdims=True) + c)
            alpha = jnp.exp2(m_old - m_new)
            p = jnp.exp2(st[...] - (m_new - c)).astype(BF16)
            o = jnp.dot(vt_ref[0, 0, j], p, preferred_element_type=F32)
            acc_ref[:, cols(s)] = alpha * acc_ref[:, cols(s)] + o
            m_ref[:, cols(s)] = m_new

        for t in range(-QK_AHEAD, len(units)):
            if t + QK_AHEAD < len(units):
                qk(t + QK_AHEAD)
            if t >= 0:
                softmax_pv(t)

    def body(jj, carry):
        run([(jj * Q_STRIPS + d, ["full"] * N_STRIPS) for d in range(Q_STRIPS)])
        return carry

    lax.fori_loop(0, i, body, 0)
    run([(i * Q_STRIPS + d,
          [None if s % Q_STRIPS < d else "diag" if s % Q_STRIPS == d else "full"
           for s in range(N_STRIPS)]) for d in range(Q_STRIPS)])

    o = acc_ref[:V_DIM, :] / acc_ref[V_DIM:V_DIM + 1, :]
    lam = (jnp.exp(jnp.sum(lq1_ref[...] * lk1_ref[...], axis=-1, keepdims=True))
           - jnp.exp(jnp.sum(lq2_ref[...] * lk2_ref[...], axis=-1, keepdims=True))
           + lam_init)
    od = o[:, :TQ] - lam * o[:, TQ:]
    ms = jnp.mean(od * od, axis=0, keepdims=True)
    y = (od * lax.rsqrt(ms + EPS)).T
    o_ref[0] = (y * g_ref[...]) * (1.0 - lam_init)


def _key_positions():
    jr = jnp.arange(TK, dtype=jnp.int32)
    lo = (jr % POS_SPLIT).astype(F32)
    hi = (jr - jr % POS_SPLIT).astype(F32)
    pos = jnp.stack([lo] * N_COEF + [hi] * N_COEF, axis=1)
    return jnp.pad(pos, ((0, 0), (0, V_DIM - 2 * N_COEF))).astype(BF16)


def _attention(q, k, vt, pos, slopes, lq1, lk1, lq2, lk2, subln_g, lam_init):
    B, S, _ = q.shape
    n_k = S // TK
    grid = (B, N_HEADS, S // TQ)
    small = lambda width: pl.BlockSpec((1, width), lambda b, h, i: (0, 0))
    return pl.pallas_call(
        functools.partial(_attn_kernel, lam_init),
        grid=grid,
        in_specs=[
            pl.BlockSpec((1, TQ, V_DIM), lambda b, h, i: (b, i, h)),
            pl.BlockSpec((1, S, V_DIM), lambda b, h, i: (b, 0, h)),
            pl.BlockSpec((1, 1, n_k, V_AUG, TK), lambda b, h, i: (b, h, 0, 0, 0)),
            pl.BlockSpec((TK, V_DIM), lambda b, h, i: (0, 0)),
            pl.BlockSpec((1, 1, 128), lambda b, h, i: (h, 0, 0)),
            small(QK_DIM), small(QK_DIM), small(QK_DIM), small(QK_DIM),
            small(V_DIM),
        ],
        out_specs=pl.BlockSpec((1, TQ, V_DIM), lambda b, h, i: (b, i, h)),
        out_shape=jax.ShapeDtypeStruct((B, S, D_ATTN), F32),
        scratch_shapes=[
            pltpu.VMEM((2 * V_DIM, 2 * TQ), BF16),
            pltpu.VMEM((1, 2 * TQ), F32),
            pltpu.VMEM((V_AUG, 2 * TQ), F32),
            pltpu.VMEM((S_SLOTS, TK, STRIP), F32),
        ],
        compiler_params=pltpu.CompilerParams(
            dimension_semantics=("arbitrary", "arbitrary", "arbitrary"),
            vmem_limit_bytes=VMEM_LIMIT),
        name="diff_attn",
    )(q, k, vt, pos, slopes, lq1, lk1, lq2, lk2, subln_g)


def _layer_norm(v, g, b):
    mu = jnp.mean(v, axis=-1, keepdims=True)
    vc = v - mu
    var = jnp.mean(vc * vc, axis=-1, keepdims=True)
    return (vc * lax.rsqrt(var + EPS)) * g + b


def _silu(v):
    return v * jax.nn.sigmoid(v)


def _mix_kernel(final, x_ref, pconv_ref, halo_ref, yb_ref, psgu_ref, gate_ref,
                cw_ref, cb_ref, clg_ref, clb_ref, slg_ref, slb_ref, ws_ref, bs_ref,
                wout_ref, nf_ref, o_ref, zbuf_ref, ybuf_ref):
    i = pl.program_id(1)

    pc = pconv_ref[0]
    zbuf_ref[HALO:, :] = pc[:, :D_CONV] * jax.nn.sigmoid(pc[:, D_CONV:])
    ph = halo_ref[0]
    zh = ph[:, :D_CONV] * jax.nn.sigmoid(ph[:, D_CONV:])
    zbuf_ref[:HALO, :] = jnp.where(i > 0, zh, 0.0)
    base = HALO - (CONV_WIDTH - 1)
    conv = jnp.zeros((TM, D_CONV), F32) + cb_ref[...]
    for b in range(SUBLANES):
        rows = TM if b == 0 else TM + SUBLANES
        part = None
        for w in range(CONV_WIDTH):
            if (base + w) % SUBLANES != b:
                continue
            term = zbuf_ref[pl.ds(base + w - b, rows), :] * cw_ref[w:w + 1, :]
            part = term if part is None else part + term
        conv = conv + (part if b == 0 else part[b:b + TM, :])
    y_a = _silu(_layer_norm(conv, clg_ref[...], clb_ref[...]))
    gate = gate_ref[0]
    ybuf_ref[:, :D_CONV] = (y_a * _silu(gate[:, :D_CONV])).astype(BF16)

    ybuf_ref[:, D_CONV:D_CONV + D_ATTN] = (
        yb_ref[0] * _silu(gate[:, D_CONV:D_CONV + D_ATTN])).astype(BF16)

    ps = psgu_ref[0]
    u = ps[:, :D_SGU]
    vv = _layer_norm(ps[:, D_SGU:], slg_ref[...], slb_ref[...]).astype(BF16)
    tr = lax.broadcasted_iota(jnp.int32, (CHUNK, CHUNK), 0)
    tc = lax.broadcasted_iota(jnp.int32, (CHUNK, CHUNK), 1)
    wm = [jnp.where(tr >= tc, ws_ref[g], 0.0).astype(BF16) for g in range(N_GROUPS)]
    lane_group = lax.broadcasted_iota(jnp.int32, (CHUNK, D_SGU), 1) // GROUP_DIM
    gate_c = _silu(gate[:, D_CONV + D_ATTN:])
    for c in range(TM // CHUNK):
        rows = slice(c * CHUNK, (c + 1) * CHUNK)
        vc = vv[rows, :]
        mixed = jnp.zeros((CHUNK, D_SGU), F32)
        for g in range(N_GROUPS):
            r = jnp.dot(wm[g], vc, preferred_element_type=F32)
            mixed = jnp.where(lane_group == g, r, mixed)
        y_c = u[rows, :] * (mixed + bs_ref[...])
        ybuf_ref[rows, D_CONV + D_ATTN:] = (y_c * gate_c[rows, :]).astype(BF16)

    out = x_ref[0] + jnp.dot(ybuf_ref[...], wout_ref[...], preferred_element_type=F32)
    if final:
        ms = jnp.mean(out * out, axis=-1, keepdims=True)
        out = (out * lax.rsqrt(ms + EPS)) * nf_ref[...]
    o_ref[0] = out


def _mix(final, x, pconv, yb, psgu, gate, cw, cb, clg, clb, slg, slb, ws, bs_exp, wout, nf):
    B, S, D = x.shape
    grid = (B, S // TM)
    row = lambda width: pl.BlockSpec((1, TM, width), lambda b, i: (b, i, 0))
    vec = lambda width: pl.BlockSpec((1, width), lambda b, i: (0, 0))
    halo_blocks = TM // HALO
    return pl.pallas_call(
        functools.partial(_mix_kernel, final),
        grid=grid,
        in_specs=[
            row(D),
            row(2 * D_CONV),
            pl.BlockSpec((1, HALO, 2 * D_CONV),
                         lambda b, i: (b, jnp.maximum(i * halo_blocks - 1, 0), 0)),
            row(D_ATTN),
            row(2 * D_SGU),
            row(D_MIX),
            pl.BlockSpec((CONV_WIDTH, D_CONV), lambda b, i: (0, 0)),
            vec(D_CONV), vec(D_CONV), vec(D_CONV),
            vec(D_SGU), vec(D_SGU),
            pl.BlockSpec((N_GROUPS, CHUNK, CHUNK), lambda b, i: (0, 0, 0)),
            pl.BlockSpec((CHUNK, D_SGU), lambda b, i: (0, 0)),
            pl.BlockSpec((D_MIX, D), lambda b, i: (0, 0)),
            vec(D),
        ],
        out_specs=row(D),
        out_shape=jax.ShapeDtypeStruct((B, S, D), F32),
        scratch_shapes=[
            pltpu.VMEM((TM + HALO, D_CONV), F32),
            pltpu.VMEM((TM, D_MIX), BF16),
        ],
        compiler_params=pltpu.CompilerParams(
            dimension_semantics=("arbitrary", "arbitrary"),
            vmem_limit_bytes=VMEM_LIMIT),
        name="mix_final" if final else "mix",
    )(x, pconv, pconv, yb, psgu, gate, cw, cb, clg, clb, slg, slb, ws, bs_exp, wout, nf)


def kernel(x, norm_g, w_in, conv_w, conv_b, conv_ln_g, conv_ln_b, lam_q1, lam_k1, lam_q2, lam_k2,
           subln_g, sgu_ln_g, sgu_ln_b, w_s, b_s, w_out, norm_f):
    depth = w_in.shape[0]
    slopes = jnp.exp2(-8.0 * (jnp.arange(N_HEADS, dtype=F32) + 1.0) / N_HEADS)
    slopes = jnp.broadcast_to(slopes[:, None, None], (N_HEADS, 1, 128))
    pos = _key_positions()
    r2 = lambda a: a.reshape(1, -1)
    for l in range(depth):
        lam_init = 0.8 - 0.6 * math.exp(-0.3 * l)
        pconv, q, k, vt, psgu, gate = _in_proj(x, r2(norm_g[l]), w_in[l].astype(BF16))
        yb = _attention(q, k, vt, pos, slopes, r2(lam_q1[l]), r2(lam_k1[l]), r2(lam_q2[l]),
                        r2(lam_k2[l]), r2(subln_g[l]), lam_init)
        bs_exp = jnp.repeat(b_s[l].T, GROUP_DIM, axis=1)
        x = _mix(l == depth - 1, x, pconv, yb, psgu, gate, conv_w[l], r2(conv_b[l]),
                 r2(conv_ln_g[l]), r2(conv_ln_b[l]), r2(sgu_ln_g[l]), r2(sgu_ln_b[l]),
                 w_s[l], bs_exp, w_out[l].astype(BF16), r2(norm_f))
    return x
```

```python
import functools
import math

import jax
import jax.numpy as jnp
from jax import lax
from jax.experimental import pallas as pl
from jax.experimental.pallas import tpu as pltpu

D_MODEL = 1024
D_MIX = D_MODEL
D_CONV = D_MIX // 4
D_ATTN = D_MIX // 2
D_SGU = D_MIX // 4
N_HEADS = 4
V_DIM = D_ATTN // N_HEADS
QK_DIM = V_DIM // 2
CONV_WIDTH = 31
N_GROUPS = 4
GROUP_DIM = D_SGU // N_GROUPS
CHUNK = 128
D_IN = 2 * D_CONV + 3 * D_ATTN + 2 * D_SGU + D_MIX
EPS = 1e-6
NEG = -1e30

C_CONV = 0
C_Q = C_CONV + 2 * D_CONV
C_K = C_Q + D_ATTN
C_V = C_K + D_ATTN
C_SGU = C_V + D_ATTN
C_GATE = C_SGU + 2 * D_SGU

TM = 512
TQ = 2048
TK = 512
STRIP = TK
Q_STRIPS = TQ // STRIP
N_STRIPS = 2 * Q_STRIPS
QK_AHEAD = 4
S_SLOTS = QK_AHEAD + 1
PV_K = 256
SUBLANES = 8
HALO = 32
V_AUG = V_DIM + 16
N_COEF = 3
POS_SPLIT = 256
LOG2E = 1.0 / math.log(2.0)
VMEM_LIMIT = 56 * 1024 * 1024

F32 = jnp.float32
BF16 = jnp.bfloat16


def _layer_norm(v, g, b):
    mu = jnp.mean(v, axis=-1, keepdims=True)
    vc = v - mu
    var = jnp.mean(vc * vc, axis=-1, keepdims=True)
    return (vc * lax.rsqrt(var + EPS)) * g + b


def _silu(v):
    return v * jax.nn.sigmoid(v)


def _in_proj_kernel(x_ref, g_ref, w_ref, slg_ref, slb_ref,
                    z_ref, q_ref, k_ref, vt_ref, u_ref, vv_ref, sgate_ref):
    x = x_ref[0]
    ms = jnp.mean(x * x, axis=-1, keepdims=True)
    h = ((x * lax.rsqrt(ms + EPS)) * g_ref[...]).astype(BF16)

    def proj(lo, width):
        return jnp.dot(h, w_ref[:, lo:lo + width], preferred_element_type=F32)

    pc = proj(C_CONV, 2 * D_CONV)
    z_ref[0] = pc[:, :D_CONV] * jax.nn.sigmoid(pc[:, D_CONV:])
    q_ref[0] = (proj(C_Q, D_ATTN) * (LOG2E / math.sqrt(QK_DIM))).astype(BF16)
    k_ref[0] = proj(C_K, D_ATTN).astype(BF16)
    v = proj(C_V, D_ATTN)
    for hd in range(N_HEADS):
        for c in range(TM // TK):
            blk = v[c * TK:(c + 1) * TK, hd * V_DIM:(hd + 1) * V_DIM]
            vt_ref[0, hd, c, :V_DIM, :] = blk.T.astype(BF16)
            vt_ref[0, hd, c, V_DIM:, :] = jnp.ones((V_AUG - V_DIM, TK), BF16)
    ps = proj(C_SGU, 2 * D_SGU)
    u_ref[0] = ps[:, :D_SGU]
    vv_ref[0] = _layer_norm(ps[:, D_SGU:], slg_ref[...], slb_ref[...]).astype(BF16)
    sgate_ref[0] = _silu(proj(C_GATE, D_MIX))


def _in_proj(x, g, w, slg, slb):
    B, S, D = x.shape
    n_k = S // TK
    grid = (B, S // TM)
    row = lambda width: pl.BlockSpec((1, TM, width), lambda b, i: (b, i, 0))
    vec = lambda width: pl.BlockSpec((1, width), lambda b, i: (0, 0))
    return pl.pallas_call(
        _in_proj_kernel,
        grid=grid,
        in_specs=[
            row(D),
            vec(D),
            pl.BlockSpec((D, D_IN), lambda b, i: (0, 0)),
            vec(D_SGU), vec(D_SGU),
        ],
        out_specs=[
            row(D_CONV),
            row(D_ATTN),
            row(D_ATTN),
            pl.BlockSpec((1, N_HEADS, TM // TK, V_AUG, TK), lambda b, i: (b, 0, i, 0, 0)),
            row(D_SGU),
            row(D_SGU),
            row(D_MIX),
        ],
        out_shape=[
            jax.ShapeDtypeStruct((B, S, D_CONV), F32),
            jax.ShapeDtypeStruct((B, S, D_ATTN), BF16),
            jax.ShapeDtypeStruct((B, S, D_ATTN), BF16),
            jax.ShapeDtypeStruct((B, N_HEADS, n_k, V_AUG, TK), BF16),
            jax.ShapeDtypeStruct((B, S, D_SGU), F32),
            jax.ShapeDtypeStruct((B, S, D_SGU), BF16),
            jax.ShapeDtypeStruct((B, S, D_MIX), F32),
        ],
        compiler_params=pltpu.CompilerParams(
            dimension_semantics=("arbitrary", "arbitrary"),
            vmem_limit_bytes=VMEM_LIMIT),
        name="in_proj",
    )(x, g, w, slg, slb)


def _attn_kernel(lam_init, q_ref, k_ref, vt_ref, pos_ref, slope_ref, lq1_ref, lk1_ref, lq2_ref,
                 lk2_ref, g_ref, o_ref, qt_ref, m_ref, acc_ref, st_ref):
    i = pl.program_id(2)
    coef = slope_ref[0][:, :1] * LOG2E

    qf = q_ref[0].astype(F32).T
    sub = lax.broadcasted_iota(jnp.int32, (V_DIM, TQ), 0)
    qt_ref[:V_DIM, :TQ] = jnp.where(sub < QK_DIM, qf, 0.0).astype(BF16)
    qt_ref[:V_DIM, TQ:] = jnp.where(sub >= QK_DIM, qf, 0.0).astype(BF16)

    c0 = coef.astype(BF16).astype(F32)
    c1 = (coef - c0).astype(BF16).astype(F32)
    c2 = (coef - c0 - c1).astype(BF16).astype(F32)
    row = lax.broadcasted_iota(jnp.int32, (V_DIM, 2 * TQ), 0)
    term = row % N_COEF
    terms = jnp.where(term == 0, c0, jnp.where(term == 1, c1, c2))
    qt_ref[V_DIM:, :] = jnp.where(row < 2 * N_COEF, terms, 0.0).astype(BF16)

    m_ref[...] = jnp.full(m_ref.shape, NEG, F32)
    acc_ref[...] = jnp.zeros(acc_ref.shape, F32)

    cols = lambda s: slice(s * STRIP, (s + 1) * STRIP)

    def key_tile(j):
        kb = k_ref[0, pl.ds(pl.multiple_of(j * TK, TK), TK), :]
        return jnp.concatenate([kb, pos_ref[...]], axis=1)

    def score(kaug, s):
        return jnp.dot(kaug, qt_ref[:, cols(s)], preferred_element_type=F32)

    def run(tiles):
        units = []
        for j, modes in tiles:
            kaug = key_tile(j)
            c = coef * (j * TK - i * TQ).astype(F32)
            units += [(j, kaug, c, s, modes[s]) for s in range(N_STRIPS) if modes[s]]

        def qk(u):
            st_ref[u % S_SLOTS] = score(units[u][1], units[u][3])

        def softmax_pv(u):
            j, _, c, s, mode = units[u]
            st = st_ref.at[u % S_SLOTS]
            if mode == "diag":
                kr = lax.broadcasted_iota(jnp.int32, (TK, STRIP), 0)
                qr = lax.broadcasted_iota(jnp.int32, (TK, STRIP), 1)
                st[...] = jnp.where(kr <= qr, st[...], NEG)
            m_old = m_ref[:, cols(s)]
            m_new = jnp.maximum(m_old, jnp.max(st[...], axis=0, keepdims=True) + c)
            alpha = jnp.exp2(m_old - m_new)
            shift = m_new - c
            o = None
            for h in range(TK // PV_K):
                keys = slice(h * PV_K, (h + 1) * PV_K)
                p = jnp.exp2(st[keys, :] - shift).astype(BF16)
                part = jnp.dot(vt_ref[0, 0, j, :, keys], p, preferred_element_type=F32)
                o = part if o is None else o + part
            acc_ref[:, cols(s)] = alpha * acc_ref[:, cols(s)] + o
            m_ref[:, cols(s)] = m_new

        for t in range(-QK_AHEAD, len(units)):
            if t + QK_AHEAD < len(units):
                qk(t + QK_AHEAD)
            if t >= 0:
                softmax_pv(t)

    def body(jj, carry):
        run([(jj * Q_STRIPS + d, ["full"] * N_STRIPS) for d in range(Q_STRIPS)])
        return carry

    lax.fori_loop(0, i, body, 0)
    run([(i * Q_STRIPS + d,
          [None if s % Q_STRIPS < d else "diag" if s % Q_STRIPS == d else "full"
           for s in range(N_STRIPS)]) for d in range(Q_STRIPS)])

    o = acc_ref[:V_DIM, :] / acc_ref[V_DIM:V_DIM + 1, :]
    lam = (jnp.exp(jnp.sum(lq1_ref[...] * lk1_ref[...], axis=-1, keepdims=True))
           - jnp.exp(jnp.sum(lq2_ref[...] * lk2_ref[...], axis=-1, keepdims=True))
           + lam_init)
    od = o[:, :TQ] - lam * o[:, TQ:]
    ms = jnp.mean(od * od, axis=0, keepdims=True)
    y = (od * lax.rsqrt(ms + EPS)).T
    o_ref[0] = (y * g_ref[...]) * (1.0 - lam_init)


def _key_positions():
    jr = jnp.arange(TK, dtype=jnp.int32)
    lo = (jr % POS_SPLIT).astype(F32)
    hi = (jr - jr % POS_SPLIT).astype(F32)
    pos = jnp.stack([lo] * N_COEF + [hi] * N_COEF, axis=1)
    return jnp.pad(pos, ((0, 0), (0, V_DIM - 2 * N_COEF))).astype(BF16)


def _attention(q, k, vt, pos, slopes, lq1, lk1, lq2, lk2, subln_g, lam_init):
    B, S, _ = q.shape
    n_k = S // TK
    grid = (B, N_HEADS, S // TQ)
    small = lambda width: pl.BlockSpec((1, width), lambda b, h, i: (0, 0))
    return pl.pallas_call(
        functools.partial(_attn_kernel, lam_init),
        grid=grid,
        in_specs=[
            pl.BlockSpec((1, TQ, V_DIM), lambda b, h, i: (b, i, h)),
            pl.BlockSpec((1, S, V_DIM), lambda b, h, i: (b, 0, h)),
            pl.BlockSpec((1, 1, n_k, V_AUG, TK), lambda b, h, i: (b, h, 0, 0, 0)),
            pl.BlockSpec((TK, V_DIM), lambda b, h, i: (0, 0)),
            pl.BlockSpec((1, 1, 128), lambda b, h, i: (h, 0, 0)),
            small(QK_DIM), small(QK_DIM), small(QK_DIM), small(QK_DIM),
            small(V_DIM),
        ],
        out_specs=pl.BlockSpec((1, TQ, V_DIM), lambda b, h, i: (b, i, h)),
        out_shape=jax.ShapeDtypeStruct((B, S, D_ATTN), F32),
        scratch_shapes=[
            pltpu.VMEM((2 * V_DIM, 2 * TQ), BF16),
            pltpu.VMEM((1, 2 * TQ), F32),
            pltpu.VMEM((V_AUG, 2 * TQ), F32),
            pltpu.VMEM((S_SLOTS, TK, STRIP), F32),
        ],
        compiler_params=pltpu.CompilerParams(
            dimension_semantics=("arbitrary", "arbitrary", "arbitrary"),
            vmem_limit_bytes=VMEM_LIMIT),
        name="diff_attn",
    )(q, k, vt, pos, slopes, lq1, lk1, lq2, lk2, subln_g)


def _mix_kernel(final, x_ref, z_ref, halo_ref, yb_ref, u_ref, vv_ref, sgate_ref,
                cw_ref, cb_ref, clg_ref, clb_ref, ws_ref, bs_ref,
                wout_ref, nf_ref, o_ref, zbuf_ref, ybuf_ref):
    i = pl.program_id(1)

    zbuf_ref[HALO:, :] = z_ref[0]
    zbuf_ref[:HALO, :] = jnp.where(i > 0, halo_ref[0], 0.0)
    base = HALO - (CONV_WIDTH - 1)
    conv = jnp.zeros((TM, D_CONV), F32) + cb_ref[...]
    for b in range(SUBLANES):
        rows = TM if b == 0 else TM + SUBLANES
        part = None
        for w in range(CONV_WIDTH):
            if (base + w) % SUBLANES != b:
                continue
            term = zbuf_ref[pl.ds(base + w - b, rows), :] * cw_ref[w:w + 1, :]
            part = term if part is None else part + term
        conv = conv + (part if b == 0 else part[b:b + TM, :])
    y_a = _silu(_layer_norm(conv, clg_ref[...], clb_ref[...]))
    ybuf_ref[:, :D_CONV] = (y_a * sgate_ref[0, :, :D_CONV]).astype(BF16)

    ybuf_ref[:, D_CONV:D_CONV + D_ATTN] = (
        yb_ref[0] * sgate_ref[0, :, D_CONV:D_CONV + D_ATTN]).astype(BF16)

    tr = lax.broadcasted_iota(jnp.int32, (CHUNK, CHUNK), 0)
    tc = lax.broadcasted_iota(jnp.int32, (CHUNK, CHUNK), 1)
    wm = [jnp.where(tr >= tc, ws_ref[g], 0.0).astype(BF16) for g in range(N_GROUPS)]
    lane_group = lax.broadcasted_iota(jnp.int32, (CHUNK, D_SGU), 1) // GROUP_DIM
    for c in range(TM // CHUNK):
        rows = slice(c * CHUNK, (c + 1) * CHUNK)
        vc = vv_ref[0, rows, :]
        mixed = jnp.zeros((CHUNK, D_SGU), F32)
        for g in range(N_GROUPS):
            r = jnp.dot(wm[g], vc, preferred_element_type=F32)
            mixed = jnp.where(lane_group == g, r, mixed)
        y_c = u_ref[0, rows, :] * (mixed + bs_ref[...])
        ybuf_ref[rows, D_CONV + D_ATTN:] = (
            y_c * sgate_ref[0, rows, D_CONV + D_ATTN:]).astype(BF16)

    out = x_ref[0] + jnp.dot(ybuf_ref[...], wout_ref[...], preferred_element_type=F32)
    if final:
        ms = jnp.mean(out * out, axis=-1, keepdims=True)
        out = (out * lax.rsqrt(ms + EPS)) * nf_ref[...]
    o_ref[0] = out


def _mix(final, x, z, yb, u, vv, sgate, cw, cb, clg, clb, ws, bs_exp, wout, nf):
    B, S, D = x.shape
    grid = (B, S // TM)
    row = lambda width: pl.BlockSpec((1, TM, width), lambda b, i: (b, i, 0))
    vec = lambda width: pl.BlockSpec((1, width), lambda b, i: (0, 0))
    halo_blocks = TM // HALO
    return pl.pallas_call(
        functools.partial(_mix_kernel, final),
        grid=grid,
        in_specs=[
            row(D),
            row(D_CONV),
            pl.BlockSpec((1, HALO, D_CONV),
                         lambda b, i: (b, jnp.maximum(i * halo_blocks - 1, 0), 0)),
            row(D_ATTN),
            row(D_SGU),
            row(D_SGU),
            row(D_MIX),
            pl.BlockSpec((CONV_WIDTH, D_CONV), lambda b, i: (0, 0)),
            vec(D_CONV), vec(D_CONV), vec(D_CONV),
            pl.BlockSpec((N_GROUPS, CHUNK, CHUNK), lambda b, i: (0, 0, 0)),
            pl.BlockSpec((CHUNK, D_SGU), lambda b, i: (0, 0)),
            pl.BlockSpec((D_MIX, D), lambda b, i: (0, 0)),
            vec(D),
        ],
        out_specs=row(D),
        out_shape=jax.ShapeDtypeStruct((B, S, D), F32),
        scratch_shapes=[
            pltpu.VMEM((TM + HALO, D_CONV), F32),
            pltpu.VMEM((TM, D_MIX), BF16),
        ],
        compiler_params=pltpu.CompilerParams(
            dimension_semantics=("arbitrary", "arbitrary"),
            vmem_limit_bytes=VMEM_LIMIT),
        name="mix_final" if final else "mix",
    )(x, z, z, yb, u, vv, sgate, cw, cb, clg, clb, ws, bs_exp, wout, nf)


def kernel(x, norm_g, w_in, conv_w, conv_b, conv_ln_g, conv_ln_b, lam_q1, lam_k1, lam_q2, lam_k2,
           subln_g, sgu_ln_g, sgu_ln_b, w_s, b_s, w_out, norm_f):
    depth = w_in.shape[0]
    slopes = jnp.exp2(-8.0 * (jnp.arange(N_HEADS, dtype=F32) + 1.0) / N_HEADS)
    slopes = jnp.broadcast_to(slopes[:, None, None], (N_HEADS, 1, 128))
    pos = _key_positions()
    r2 = lambda a: a.reshape(1, -1)
    for l in range(depth):
        lam_init = 0.8 - 0.6 * math.exp(-0.3 * l)
        z, q, k, vt, u, vv, sgate = _in_proj(x, r2(norm_g[l]), w_in[l].astype(BF16),
                                             r2(sgu_ln_g[l]), r2(sgu_ln_b[l]))
        yb = _attention(q, k, vt, pos, slopes, r2(lam_q1[l]), r2(lam_k1[l]), r2(lam_q2[l]),
                        r2(lam_k2[l]), r2(subln_g[l]), lam_init)
        bs_exp = jnp.repeat(b_s[l].T, GROUP_DIM, axis=1)
        x = _mix(l == depth - 1, x, z, yb, u, vv, sgate, conv_w[l], r2(conv_b[l]),
                 r2(conv_ln_g[l]), r2(conv_ln_b[l]), w_s[l], bs_exp, w_out[l].astype(BF16),
                 r2(norm_f))
    return x
```

```python
import functools
import math

import jax
import jax.numpy as jnp
from jax import lax
from jax.experimental import pallas as pl
from jax.experimental.pallas import tpu as pltpu

D_MODEL = 1024
D_MIX = D_MODEL
D_CONV = D_MIX // 4
D_ATTN = D_MIX // 2
D_SGU = D_MIX // 4
N_HEADS = 4
V_DIM = D_ATTN // N_HEADS
QK_DIM = V_DIM // 2
CONV_WIDTH = 31
N_GROUPS = 4
GROUP_DIM = D_SGU // N_GROUPS
CHUNK = 128
D_IN = 2 * D_CONV + 3 * D_ATTN + 2 * D_SGU + D_MIX
EPS = 1e-6
NEG = -1e30

C_CONV = 0
C_Q = C_CONV + 2 * D_CONV
C_K = C_Q + D_ATTN
C_V = C_K + D_ATTN
C_SGU = C_V + D_ATTN
C_GATE = C_SGU + 2 * D_SGU

TM = 512
TQ = 2048
TK = 512
STRIP = TK
Q_STRIPS = TQ // STRIP
N_STRIPS = 2 * Q_STRIPS
QK_AHEAD = 4
S_SLOTS = QK_AHEAD + 1
PV_K = 256
SUBLANES = 8
HALO = 32
CONV_ROWS = TM // 4
V_AUG = V_DIM + 16
N_COEF = 3
POS_SPLIT = 256
LOG2E = 1.0 / math.log(2.0)
VMEM_LIMIT = 56 * 1024 * 1024

F32 = jnp.float32
BF16 = jnp.bfloat16


def _layer_norm(v, g, b):
    mu = jnp.mean(v, axis=-1, keepdims=True)
    vc = v - mu
    var = jnp.mean(vc * vc, axis=-1, keepdims=True)
    return (vc * lax.rsqrt(var + EPS)) * g + b


def _silu(v):
    return v * jax.nn.sigmoid(v)


def _causal_conv(zbuf_ref, cw_ref, bias, r0, n):
    base = HALO - (CONV_WIDTH - 1)
    conv = jnp.zeros((n, D_CONV), F32) + bias
    for b in range(SUBLANES):
        rows = n if b == 0 else n + SUBLANES
        part = None
        for w in range(CONV_WIDTH):
            if (base + w) % SUBLANES != b:
                continue
            term = zbuf_ref[pl.ds(r0 + base + w - b, rows), :] * cw_ref[w:w + 1, :]
            part = term if part is None else part + term
        conv = conv + (part if b == 0 else part[b:b + n, :])
    return conv


def _zero_after(v):
    bits = pltpu.bitcast(v[:SUBLANES, -D_CONV:], jnp.uint32)
    return ((bits >> 16) >> 16)[:1, :].astype(F32)


def _in_proj_kernel(x_ref, g_ref, w_ref, slg_ref, slb_ref, cw_ref, cb_ref, clg_ref, clb_ref,
                    ya_ref, q_ref, k_ref, vt_ref, u_ref, vv_ref, sgate_ref, zbuf_ref,
                    sgate_a_ref):
    x = x_ref[0]
    ms = jnp.mean(x * x, axis=-1, keepdims=True)
    h = ((x * lax.rsqrt(ms + EPS)) * g_ref[...]).astype(BF16)

    def proj(lo, width):
        return jnp.dot(h, w_ref[:, lo:lo + width], preferred_element_type=F32)

    @pl.when(pl.program_id(1) == 0)
    def _():
        zbuf_ref[:HALO, :] = jnp.zeros((HALO, D_CONV), F32)

    def conv_chunk(c, after):
        r0, n = c * CONV_ROWS, CONV_ROWS
        conv = _causal_conv(zbuf_ref, cw_ref, cb_ref[...] + _zero_after(after), r0, n)
        y_a = _silu(_layer_norm(conv, clg_ref[...], clb_ref[...]))
        ya_ref[0, r0:r0 + n, :] = (y_a * sgate_a_ref[r0:r0 + n, :]).astype(BF16)

    pc = proj(C_CONV, 2 * D_CONV)
    zbuf_ref[HALO:, :] = pc[:, :D_CONV] * jax.nn.sigmoid(pc[:, D_CONV:])
    ga = proj(C_GATE, D_CONV)
    sgate_a_ref[...] = _silu(ga)
    conv_chunk(0, ga)
    qp = proj(C_Q, D_ATTN)
    q_ref[0] = (qp * (LOG2E / math.sqrt(QK_DIM))).astype(BF16)
    conv_chunk(1, qp)
    kp = proj(C_K, D_ATTN)
    k_ref[0] = kp.astype(BF16)
    conv_chunk(2, kp)
    v = proj(C_V, D_ATTN)
    for hd in range(N_HEADS):
        for c in range(TM // TK):
            blk = v[c * TK:(c + 1) * TK, hd * V_DIM:(hd + 1) * V_DIM]
            vt_ref[0, hd, c, :V_DIM, :] = blk.T.astype(BF16)
            vt_ref[0, hd, c, V_DIM:, :] = jnp.ones((V_AUG - V_DIM, TK), BF16)
    conv_chunk(3, v)
    ps = proj(C_SGU, 2 * D_SGU)
    u_ref[0] = ps[:, :D_SGU]
    vv_ref[0] = _layer_norm(ps[:, D_SGU:], slg_ref[...], slb_ref[...]).astype(BF16)
    sgate_ref[0] = _silu(proj(C_GATE + D_CONV, D_MIX - D_CONV))
    zbuf_ref[:HALO, :] = zbuf_ref[TM:, :]


def _in_proj(x, g, w, slg, slb, cw, cb, clg, clb):
    B, S, D = x.shape
    n_k = S // TK
    grid = (B, S // TM)
    row = lambda width: pl.BlockSpec((1, TM, width), lambda b, i: (b, i, 0))
    vec = lambda width: pl.BlockSpec((1, width), lambda b, i: (0, 0))
    return pl.pallas_call(
        _in_proj_kernel,
        grid=grid,
        in_specs=[
            row(D),
            vec(D),
            pl.BlockSpec((D, D_IN), lambda b, i: (0, 0)),
            vec(D_SGU), vec(D_SGU),
            pl.BlockSpec((CONV_WIDTH, D_CONV), lambda b, i: (0, 0)),
            vec(D_CONV), vec(D_CONV), vec(D_CONV),
        ],
        out_specs=[
            row(D_CONV),
            row(D_ATTN),
            row(D_ATTN),
            pl.BlockSpec((1, N_HEADS, TM // TK, V_AUG, TK), lambda b, i: (b, 0, i, 0, 0)),
            row(D_SGU),
            row(D_SGU),
            row(D_MIX - D_CONV),
        ],
        out_shape=[
            jax.ShapeDtypeStruct((B, S, D_CONV), BF16),
            jax.ShapeDtypeStruct((B, S, D_ATTN), BF16),
            jax.ShapeDtypeStruct((B, S, D_ATTN), BF16),
            jax.ShapeDtypeStruct((B, N_HEADS, n_k, V_AUG, TK), BF16),
            jax.ShapeDtypeStruct((B, S, D_SGU), F32),
            jax.ShapeDtypeStruct((B, S, D_SGU), BF16),
            jax.ShapeDtypeStruct((B, S, D_MIX - D_CONV), F32),
        ],
        scratch_shapes=[
            pltpu.VMEM((TM + HALO, D_CONV), F32),
            pltpu.VMEM((TM, D_CONV), F32),
        ],
        compiler_params=pltpu.CompilerParams(
            dimension_semantics=("arbitrary", "arbitrary"),
            vmem_limit_bytes=VMEM_LIMIT),
        name="in_proj",
    )(x, g, w, slg, slb, cw, cb, clg, clb)


def _attn_kernel(lam_init, q_ref, k_ref, vt_ref, pos_ref, slope_ref, lq1_ref, lk1_ref, lq2_ref,
                 lk2_ref, g_ref, o_ref, qt_ref, m_ref, acc_ref, st_ref):
    i = pl.program_id(2)
    coef = slope_ref[0][:, :1] * LOG2E

    qf = q_ref[0].astype(F32).T
    sub = lax.broadcasted_iota(jnp.int32, (V_DIM, TQ), 0)
    qt_ref[:V_DIM, :TQ] = jnp.where(sub < QK_DIM, qf, 0.0).astype(BF16)
    qt_ref[:V_DIM, TQ:] = jnp.where(sub >= QK_DIM, qf, 0.0).astype(BF16)

    c0 = coef.astype(BF16).astype(F32)
    c1 = (coef - c0).astype(BF16).astype(F32)
    c2 = (coef - c0 - c1).astype(BF16).astype(F32)
    row = lax.broadcasted_iota(jnp.int32, (V_DIM, 2 * TQ), 0)
    term = row % N_COEF
    terms = jnp.where(term == 0, c0, jnp.where(term == 1, c1, c2))
    qt_ref[V_DIM:, :] = jnp.where(row < 2 * N_COEF, terms, 0.0).astype(BF16)

    m_ref[...] = jnp.full(m_ref.shape, NEG, F32)
    acc_ref[...] = jnp.zeros(acc_ref.shape, F32)

    cols = lambda s: slice(s * STRIP, (s + 1) * STRIP)

    def key_tile(j):
        kb = k_ref[0, pl.ds(pl.multiple_of(j * TK, TK), TK), :]
        return jnp.concatenate([kb, pos_ref[...]], axis=1)

    def score(kaug, s):
        return jnp.dot(kaug, qt_ref[:, cols(s)], preferred_element_type=F32)

    def run(tiles):
        units = []
        for j, modes in tiles:
            kaug = key_tile(j)
            c = coef * (j * TK - i * TQ).astype(F32)
            units += [(j, kaug, c, s, modes[s]) for s in range(N_STRIPS) if modes[s]]

        def qk(u):
            st_ref[u % S_SLOTS] = score(units[u][1], units[u][3])

        def softmax_pv(u):
            j, _, c, s, mode = units[u]
            st = st_ref.at[u % S_SLOTS]
            if mode == "diag":
                kr = lax.broadcasted_iota(jnp.int32, (TK, STRIP), 0)
                qr = lax.broadcasted_iota(jnp.int32, (TK, STRIP), 1)
                st[...] = jnp.where(kr <= qr, st[...], NEG)
            m_old = m_ref[:, cols(s)]
            m_new = jnp.maximum(m_old, jnp.max(st[...], axis=0, keepdims=True) + c)
            alpha = jnp.exp2(m_old - m_new)
            shift = m_new - c
            o = None
            for h in range(TK // PV_K):
                keys = slice(h * PV_K, (h + 1) * PV_K)
                p = jnp.exp2(st[keys, :] - shift).astype(BF16)
                part = jnp.dot(vt_ref[0, 0, j, :, keys], p, preferred_element_type=F32)
                o = part if o is None else o + part
            acc_ref[:, cols(s)] = alpha * acc_ref[:, cols(s)] + o
            m_ref[:, cols(s)] = m_new

        for t in range(-QK_AHEAD, len(units)):
            if t + QK_AHEAD < len(units):
                qk(t + QK_AHEAD)
            if t >= 0:
                softmax_pv(t)

    def body(jj, carry):
        run([(jj * Q_STRIPS + d, ["full"] * N_STRIPS) for d in range(Q_STRIPS)])
        return carry

    lax.fori_loop(0, i, body, 0)
    run([(i * Q_STRIPS + d,
          [None if s % Q_STRIPS < d else "diag" if s % Q_STRIPS == d else "full"
           for s in range(N_STRIPS)]) for d in range(Q_STRIPS)])

    o = acc_ref[:V_DIM, :] / acc_ref[V_DIM:V_DIM + 1, :]
    lam = (jnp.exp(jnp.sum(lq1_ref[...] * lk1_ref[...], axis=-1, keepdims=True))
           - jnp.exp(jnp.sum(lq2_ref[...] * lk2_ref[...], axis=-1, keepdims=True))
           + lam_init)
    od = o[:, :TQ] - lam * o[:, TQ:]
    ms = jnp.mean(od * od, axis=0, keepdims=True)
    y = (od * lax.rsqrt(ms + EPS)).T
    o_ref[0] = (y * g_ref[...]) * (1.0 - lam_init)


def _key_positions():
    jr = jnp.arange(TK, dtype=jnp.int32)
    lo = (jr % POS_SPLIT).astype(F32)
    hi = (jr - jr % POS_SPLIT).astype(F32)
    pos = jnp.stack([lo] * N_COEF + [hi] * N_COEF, axis=1)
    return jnp.pad(pos, ((0, 0), (0, V_DIM - 2 * N_COEF))).astype(BF16)


def _attention(q, k, vt, pos, slopes, lq1, lk1, lq2, lk2, subln_g, lam_init):
    B, S, _ = q.shape
    n_k = S // TK
    grid = (B, N_HEADS, S // TQ)
    small = lambda width: pl.BlockSpec((1, width), lambda b, h, i: (0, 0))
    return pl.pallas_call(
        functools.partial(_attn_kernel, lam_init),
        grid=grid,
        in_specs=[
            pl.BlockSpec((1, TQ, V_DIM), lambda b, h, i: (b, i, h)),
            pl.BlockSpec((1, S, V_DIM), lambda b, h, i: (b, 0, h)),
            pl.BlockSpec((1, 1, n_k, V_AUG, TK), lambda b, h, i: (b, h, 0, 0, 0)),
            pl.BlockSpec((TK, V_DIM), lambda b, h, i: (0, 0)),
            pl.BlockSpec((1, 1, 128), lambda b, h, i: (h, 0, 0)),
            small(QK_DIM), small(QK_DIM), small(QK_DIM), small(QK_DIM),
            small(V_DIM),
        ],
        out_specs=pl.BlockSpec((1, TQ, V_DIM), lambda b, h, i: (b, i, h)),
        out_shape=jax.ShapeDtypeStruct((B, S, D_ATTN), F32),
        scratch_shapes=[
            pltpu.VMEM((2 * V_DIM, 2 * TQ), BF16),
            pltpu.VMEM((1, 2 * TQ), F32),
            pltpu.VMEM((V_AUG, 2 * TQ), F32),
            pltpu.VMEM((S_SLOTS, TK, STRIP), F32),
        ],
        compiler_params=pltpu.CompilerParams(
            dimension_semantics=("arbitrary", "arbitrary", "arbitrary"),
            vmem_limit_bytes=VMEM_LIMIT),
        name="diff_attn",
    )(q, k, vt, pos, slopes, lq1, lk1, lq2, lk2, subln_g)


def _mix_kernel(final, x_ref, ya_ref, yb_ref, u_ref, vv_ref, sgate_ref, ws_ref, bs_ref,
                wout_ref, nf_ref, o_ref, ybuf_ref):
    ybuf_ref[:, :D_CONV] = ya_ref[0]

    ybuf_ref[:, D_CONV:D_CONV + D_ATTN] = (yb_ref[0] * sgate_ref[0, :, :D_ATTN]).astype(BF16)

    tr = lax.broadcasted_iota(jnp.int32, (CHUNK, CHUNK), 0)
    tc = lax.broadcasted_iota(jnp.int32, (CHUNK, CHUNK), 1)
    wm = [jnp.where(tr >= tc, ws_ref[g], 0.0).astype(BF16) for g in range(N_GROUPS)]
    lane_group = lax.broadcasted_iota(jnp.int32, (CHUNK, D_SGU), 1) // GROUP_DIM
    for c in range(TM // CHUNK):
        rows = slice(c * CHUNK, (c + 1) * CHUNK)
        vc = vv_ref[0, rows, :]
        mixed = jnp.zeros((CHUNK, D_SGU), F32)
        for g in range(N_GROUPS):
            r = jnp.dot(wm[g], vc, preferred_element_type=F32)
            mixed = jnp.where(lane_group == g, r, mixed)
        y_c = u_ref[0, rows, :] * (mixed + bs_ref[...])
        ybuf_ref[rows, D_CONV + D_ATTN:] = (y_c * sgate_ref[0, rows, D_ATTN:]).astype(BF16)

    out = x_ref[0] + jnp.dot(ybuf_ref[...], wout_ref[...], preferred_element_type=F32)
    if final:
        ms = jnp.mean(out * out, axis=-1, keepdims=True)
        out = (out * lax.rsqrt(ms + EPS)) * nf_ref[...]
    o_ref[0] = out


def _mix(final, x, ya, yb, u, vv, sgate, ws, bs_exp, wout, nf):
    B, S, D = x.shape
    grid = (B, S // TM)
    row = lambda width: pl.BlockSpec((1, TM, width), lambda b, i: (b, i, 0))
    vec = lambda width: pl.BlockSpec((1, width), lambda b, i: (0, 0))
    return pl.pallas_call(
        functools.partial(_mix_kernel, final),
        grid=grid,
        in_specs=[
            row(D),
            row(D_CONV),
            row(D_ATTN),
            row(D_SGU),
            row(D_SGU),
            row(D_MIX - D_CONV),
            pl.BlockSpec((N_GROUPS, CHUNK, CHUNK), lambda b, i: (0, 0, 0)),
            pl.BlockSpec((CHUNK, D_SGU), lambda b, i: (0, 0)),
            pl.BlockSpec((D_MIX, D), lambda b, i: (0, 0)),
            vec(D),
        ],
        out_specs=row(D),
        out_shape=jax.ShapeDtypeStruct((B, S, D), F32),
        scratch_shapes=[
            pltpu.VMEM((TM, D_MIX), BF16),
        ],
        compiler_params=pltpu.CompilerParams(
            dimension_semantics=("arbitrary", "arbitrary"),
            vmem_limit_bytes=VMEM_LIMIT),
        name="mix_final" if final else "mix",
    )(x, ya, yb, u, vv, sgate, ws, bs_exp, wout, nf)


def kernel(x, norm_g, w_in, conv_w, conv_b, conv_ln_g, conv_ln_b, lam_q1, lam_k1, lam_q2, lam_k2,
           subln_g, sgu_ln_g, sgu_ln_b, w_s, b_s, w_out, norm_f):
    depth = w_in.shape[0]
    slopes = jnp.exp2(-8.0 * (jnp.arange(N_HEADS, dtype=F32) + 1.0) / N_HEADS)
    slopes = jnp.broadcast_to(slopes[:, None, None], (N_HEADS, 1, 128))
    pos = _key_positions()
    r2 = lambda a: a.reshape(1, -1)
    for l in range(depth):
        lam_init = 0.8 - 0.6 * math.exp(-0.3 * l)
        ya, q, k, vt, u, vv, sgate = _in_proj(
            x, r2(norm_g[l]), w_in[l].astype(BF16), r2(sgu_ln_g[l]), r2(sgu_ln_b[l]),
            conv_w[l], r2(conv_b[l]), r2(conv_ln_g[l]), r2(conv_ln_b[l]))
        yb = _attention(q, k, vt, pos, slopes, r2(lam_q1[l]), r2(lam_k1[l]), r2(lam_q2[l]),
                        r2(lam_k2[l]), r2(subln_g[l]), lam_init)
        bs_exp = jnp.repeat(b_s[l].T, GROUP_DIM, axis=1)
        x = _mix(l == depth - 1, x, ya, yb, u, vv, sgate, w_s[l], bs_exp,
                 w_out[l].astype(BF16), r2(norm_f))
    return x
```

```python
import functools
import math

import jax
import jax.numpy as jnp
from jax import lax
from jax.experimental import pallas as pl
from jax.experimental.pallas import tpu as pltpu

D_MODEL = 1024
D_MIX = D_MODEL
D_CONV = D_MIX // 4
D_ATTN = D_MIX // 2
D_SGU = D_MIX // 4
N_HEADS = 4
V_DIM = D_ATTN // N_HEADS
QK_DIM = V_DIM // 2
CONV_WIDTH = 31
N_GROUPS = 4
GROUP_DIM = D_SGU // N_GROUPS
CHUNK = 128
D_IN = 2 * D_CONV + 3 * D_ATTN + 2 * D_SGU + D_MIX
EPS = 1e-6
NEG = -1e30

C_CONV = 0
C_Q = C_CONV + 2 * D_CONV
C_K = C_Q + D_ATTN
C_V = C_K + D_ATTN
C_SGU = C_V + D_ATTN
C_GATE = C_SGU + 2 * D_SGU

TM = 512
TQ = 2048
TK = 512
STRIP = TK
Q_STRIPS = TQ // STRIP
N_STRIPS = 2 * Q_STRIPS
QK_AHEAD = 4
S_SLOTS = QK_AHEAD + 1
PV_K = 256
SUBLANES = 8
HALO = 32
CONV_ROWS = TM // 4
V_AUG = V_DIM + 16
N_COEF = 3
POS_SPLIT = 256
LOG2E = 1.0 / math.log(2.0)
VMEM_LIMIT = 56 * 1024 * 1024

F32 = jnp.float32
BF16 = jnp.bfloat16


def _layer_norm(v, g, b):
    mu = jnp.mean(v, axis=-1, keepdims=True)
    vc = v - mu
    var = jnp.mean(vc * vc, axis=-1, keepdims=True)
    return (vc * lax.rsqrt(var + EPS)) * g + b


def _silu(v):
    return v * jax.nn.sigmoid(v)


def _causal_conv(zbuf_ref, cw_ref, bias, r0, n):
    base = HALO - (CONV_WIDTH - 1)
    conv = jnp.zeros((n, D_CONV), F32) + bias
    for b in range(SUBLANES):
        rows = n if b == 0 else n + SUBLANES
        part = None
        for w in range(CONV_WIDTH):
            if (base + w) % SUBLANES != b:
                continue
            term = zbuf_ref[pl.ds(r0 + base + w - b, rows), :] * cw_ref[w:w + 1, :]
            part = term if part is None else part + term
        conv = conv + (part if b == 0 else part[b:b + n, :])
    return conv


def _zero_after(v):
    bits = pltpu.bitcast(v[:SUBLANES, -D_CONV:], jnp.uint32)
    return ((bits >> 16) >> 16)[:1, :].astype(F32)


def _in_proj_kernel(x_ref, g_ref, w_ref, slg_ref, slb_ref, cw_ref, cb_ref, clg_ref, clb_ref,
                    ya_ref, q_ref, k_ref, vt_ref, u_ref, vv_ref, sgate_ref, zbuf_ref,
                    sgate_a_ref):
    x = x_ref[0]
    ms = jnp.mean(x * x, axis=-1, keepdims=True)
    h = ((x * lax.rsqrt(ms + EPS)) * g_ref[...]).astype(BF16)

    def proj(lo, width):
        return jnp.dot(h, w_ref[:, lo:lo + width], preferred_element_type=F32)

    @pl.when(pl.program_id(1) == 0)
    def _():
        zbuf_ref[:HALO, :] = jnp.zeros((HALO, D_CONV), F32)

    def conv_chunk(c, after):
        r0, n = c * CONV_ROWS, CONV_ROWS
        conv = _causal_conv(zbuf_ref, cw_ref, cb_ref[...] + _zero_after(after), r0, n)
        y_a = _silu(_layer_norm(conv, clg_ref[...], clb_ref[...]))
        ya_ref[0, r0:r0 + n, :] = (y_a * sgate_a_ref[r0:r0 + n, :]).astype(BF16)

    pc = proj(C_CONV, 2 * D_CONV)
    zbuf_ref[HALO:, :] = pc[:, :D_CONV] * jax.nn.sigmoid(pc[:, D_CONV:])
    ga = proj(C_GATE, D_CONV)
    sgate_a_ref[...] = _silu(ga)
    conv_chunk(0, ga)
    qp = proj(C_Q, D_ATTN)
    q_ref[0] = (qp * (LOG2E / math.sqrt(QK_DIM))).astype(BF16)
    conv_chunk(1, qp)
    kp = proj(C_K, D_ATTN)
    k_ref[0] = kp.astype(BF16)
    conv_chunk(2, kp)
    v = proj(C_V, D_ATTN)
    for hd in range(N_HEADS):
        for c in range(TM // TK):
            blk = v[c * TK:(c + 1) * TK, hd * V_DIM:(hd + 1) * V_DIM]
            vt_ref[0, hd, c, :V_DIM, :] = blk.T.astype(BF16)
            vt_ref[0, hd, c, V_DIM:, :] = jnp.ones((V_AUG - V_DIM, TK), BF16)
    conv_chunk(3, v)
    ps = proj(C_SGU, 2 * D_SGU)
    u_ref[0] = ps[:, :D_SGU]
    vv_ref[0] = _layer_norm(ps[:, D_SGU:], slg_ref[...], slb_ref[...]).astype(BF16)
    sgate_ref[0] = _silu(proj(C_GATE + D_CONV, D_MIX - D_CONV)).astype(BF16)
    zbuf_ref[:HALO, :] = zbuf_ref[TM:, :]


def _in_proj(x, g, w, slg, slb, cw, cb, clg, clb):
    B, S, D = x.shape
    n_k = S // TK
    grid = (B, S // TM)
    row = lambda width: pl.BlockSpec((1, TM, width), lambda b, i: (b, i, 0))
    vec = lambda width: pl.BlockSpec((1, width), lambda b, i: (0, 0))
    return pl.pallas_call(
        _in_proj_kernel,
        grid=grid,
        in_specs=[
            row(D),
            vec(D),
            pl.BlockSpec((D, D_IN), lambda b, i: (0, 0)),
            vec(D_SGU), vec(D_SGU),
            pl.BlockSpec((CONV_WIDTH, D_CONV), lambda b, i: (0, 0)),
            vec(D_CONV), vec(D_CONV), vec(D_CONV),
        ],
        out_specs=[
            row(D_CONV),
            row(D_ATTN),
            row(D_ATTN),
            pl.BlockSpec((1, N_HEADS, TM // TK, V_AUG, TK), lambda b, i: (b, 0, i, 0, 0)),
            row(D_SGU),
            row(D_SGU),
            row(D_MIX - D_CONV),
        ],
        out_shape=[
            jax.ShapeDtypeStruct((B, S, D_CONV), BF16),
            jax.ShapeDtypeStruct((B, S, D_ATTN), BF16),
            jax.ShapeDtypeStruct((B, S, D_ATTN), BF16),
            jax.ShapeDtypeStruct((B, N_HEADS, n_k, V_AUG, TK), BF16),
            jax.ShapeDtypeStruct((B, S, D_SGU), F32),
            jax.ShapeDtypeStruct((B, S, D_SGU), BF16),
            jax.ShapeDtypeStruct((B, S, D_MIX - D_CONV), BF16),
        ],
        scratch_shapes=[
            pltpu.VMEM((TM + HALO, D_CONV), F32),
            pltpu.VMEM((TM, D_CONV), F32),
        ],
        compiler_params=pltpu.CompilerParams(
            dimension_semantics=("arbitrary", "arbitrary"),
            vmem_limit_bytes=VMEM_LIMIT),
        name="in_proj",
    )(x, g, w, slg, slb, cw, cb, clg, clb)


def _attn_kernel(lam_init, q_ref, k_ref, vt_ref, pos_ref, slope_ref, lq1_ref, lk1_ref, lq2_ref,
                 lk2_ref, g_ref, o_ref, qt_ref, m_ref, acc_ref, st_ref):
    i = pl.program_id(2)
    coef = slope_ref[0][:, :1] * LOG2E

    qf = q_ref[0].astype(F32).T
    sub = lax.broadcasted_iota(jnp.int32, (V_DIM, TQ), 0)
    qt_ref[:V_DIM, :TQ] = jnp.where(sub < QK_DIM, qf, 0.0).astype(BF16)
    qt_ref[:V_DIM, TQ:] = jnp.where(sub >= QK_DIM, qf, 0.0).astype(BF16)

    c0 = coef.astype(BF16).astype(F32)
    c1 = (coef - c0).astype(BF16).astype(F32)
    c2 = (coef - c0 - c1).astype(BF16).astype(F32)
    row = lax.broadcasted_iota(jnp.int32, (V_DIM, 2 * TQ), 0)
    term = row % N_COEF
    terms = jnp.where(term == 0, c0, jnp.where(term == 1, c1, c2))
    qt_ref[V_DIM:, :] = jnp.where(row < 2 * N_COEF, terms, 0.0).astype(BF16)

    m_ref[...] = jnp.full(m_ref.shape, NEG, F32)
    acc_ref[...] = jnp.zeros(acc_ref.shape, F32)

    cols = lambda s: slice(s * STRIP, (s + 1) * STRIP)

    def key_tile(j):
        kb = k_ref[0, pl.ds(pl.multiple_of(j * TK, TK), TK), :]
        return jnp.concatenate([kb, pos_ref[...]], axis=1)

    def score(kaug, s):
        return jnp.dot(kaug, qt_ref[:, cols(s)], preferred_element_type=F32)

    def run(tiles):
        units = []
        for j, modes in tiles:
            kaug = key_tile(j)
            c = coef * (j * TK - i * TQ).astype(F32)
            units += [(j, kaug, c, s, modes[s]) for s in range(N_STRIPS) if modes[s]]

        def qk(u):
            st_ref[u % S_SLOTS] = score(units[u][1], units[u][3])

        def softmax_pv(u):
            j, _, c, s, mode = units[u]
            st = st_ref.at[u % S_SLOTS]
            if mode == "diag":
                kr = lax.broadcasted_iota(jnp.int32, (TK, STRIP), 0)
                qr = lax.broadcasted_iota(jnp.int32, (TK, STRIP), 1)
                st[...] = jnp.where(kr <= qr, st[...], NEG)
            m_old = m_ref[:, cols(s)]
            m_new = jnp.maximum(m_old, jnp.max(st[...], axis=0, keepdims=True) + c)
            alpha = jnp.exp2(m_old - m_new)
            shift = m_new - c
            o = None
            for h in range(TK // PV_K):
                keys = slice(h * PV_K, (h + 1) * PV_K)
                p = jnp.exp2(st[keys, :] - shift).astype(BF16)
                part = jnp.dot(vt_ref[0, 0, j, :, keys], p, preferred_element_type=F32)
                o = part if o is None else o + part
            acc_ref[:, cols(s)] = alpha * acc_ref[:, cols(s)] + o
            m_ref[:, cols(s)] = m_new

        for t in range(-QK_AHEAD, len(units)):
            if t + QK_AHEAD < len(units):
                qk(t + QK_AHEAD)
            if t >= 0:
                softmax_pv(t)

    def body(jj, carry):
        run([(jj * Q_STRIPS + d, ["full"] * N_STRIPS) for d in range(Q_STRIPS)])
        return carry

    lax.fori_loop(0, i, body, 0)
    run([(i * Q_STRIPS + d,
          [None if s % Q_STRIPS < d else "diag" if s % Q_STRIPS == d else "full"
           for s in range(N_STRIPS)]) for d in range(Q_STRIPS)])

    o = acc_ref[:V_DIM, :] / acc_ref[V_DIM:V_DIM + 1, :]
    lam = (jnp.exp(jnp.sum(lq1_ref[...] * lk1_ref[...], axis=-1, keepdims=True))
           - jnp.exp(jnp.sum(lq2_ref[...] * lk2_ref[...], axis=-1, keepdims=True))
           + lam_init)
    od = o[:, :TQ] - lam * o[:, TQ:]
    ms = jnp.mean(od * od, axis=0, keepdims=True)
    y = (od * lax.rsqrt(ms + EPS)).T
    o_ref[0] = ((y * g_ref[...]) * (1.0 - lam_init)).astype(BF16)


def _key_positions():
    jr = jnp.arange(TK, dtype=jnp.int32)
    lo = (jr % POS_SPLIT).astype(F32)
    hi = (jr - jr % POS_SPLIT).astype(F32)
    pos = jnp.stack([lo] * N_COEF + [hi] * N_COEF, axis=1)
    return jnp.pad(pos, ((0, 0), (0, V_DIM - 2 * N_COEF))).astype(BF16)


def _attention(q, k, vt, pos, slopes, lq1, lk1, lq2, lk2, subln_g, lam_init):
    B, S, _ = q.shape
    n_k = S // TK
    grid = (B, N_HEADS, S // TQ)
    small = lambda width: pl.BlockSpec((1, width), lambda b, h, i: (0, 0))
    return pl.pallas_call(
        functools.partial(_attn_kernel, lam_init),
        grid=grid,
        in_specs=[
            pl.BlockSpec((1, TQ, V_DIM), lambda b, h, i: (b, i, h)),
            pl.BlockSpec((1, S, V_DIM), lambda b, h, i: (b, 0, h)),
            pl.BlockSpec((1, 1, n_k, V_AUG, TK), lambda b, h, i: (b, h, 0, 0, 0)),
            pl.BlockSpec((TK, V_DIM), lambda b, h, i: (0, 0)),
            pl.BlockSpec((1, 1, 128), lambda b, h, i: (h, 0, 0)),
            small(QK_DIM), small(QK_DIM), small(QK_DIM), small(QK_DIM),
            small(V_DIM),
        ],
        out_specs=pl.BlockSpec((1, TQ, V_DIM), lambda b, h, i: (b, i, h)),
        out_shape=jax.ShapeDtypeStruct((B, S, D_ATTN), BF16),
        scratch_shapes=[
            pltpu.VMEM((2 * V_DIM, 2 * TQ), BF16),
            pltpu.VMEM((1, 2 * TQ), F32),
            pltpu.VMEM((V_AUG, 2 * TQ), F32),
            pltpu.VMEM((S_SLOTS, TK, STRIP), F32),
        ],
        compiler_params=pltpu.CompilerParams(
            dimension_semantics=("arbitrary", "arbitrary", "arbitrary"),
            vmem_limit_bytes=VMEM_LIMIT),
        name="diff_attn",
    )(q, k, vt, pos, slopes, lq1, lk1, lq2, lk2, subln_g)


def _mix_kernel(final, x_ref, ya_ref, yb_ref, u_ref, vv_ref, sgate_ref, ws_ref, bs_ref,
                wout_ref, nf_ref, o_ref, ybuf_ref):
    ybuf_ref[:, :D_CONV] = ya_ref[0]

    ybuf_ref[:, D_CONV:D_CONV + D_ATTN] = yb_ref[0] * sgate_ref[0, :, :D_ATTN]

    tr = lax.broadcasted_iota(jnp.int32, (CHUNK, CHUNK), 0)
    tc = lax.broadcasted_iota(jnp.int32, (CHUNK, CHUNK), 1)
    wm = [jnp.where(tr >= tc, ws_ref[g], 0.0).astype(BF16) for g in range(N_GROUPS)]
    lane_group = lax.broadcasted_iota(jnp.int32, (CHUNK, D_SGU), 1) // GROUP_DIM
    for c in range(TM // CHUNK):
        rows = slice(c * CHUNK, (c + 1) * CHUNK)
        vc = vv_ref[0, rows, :]
        mixed = jnp.zeros((CHUNK, D_SGU), F32)
        for g in range(N_GROUPS):
            r = jnp.dot(wm[g], vc, preferred_element_type=F32)
            mixed = jnp.where(lane_group == g, r, mixed)
        y_c = u_ref[0, rows, :] * (mixed + bs_ref[...])
        ybuf_ref[rows, D_CONV + D_ATTN:] = (
            y_c * sgate_ref[0, rows, D_ATTN:].astype(F32)).astype(BF16)

    out = x_ref[0] + jnp.dot(ybuf_ref[...], wout_ref[...], preferred_element_type=F32)
    if final:
        ms = jnp.mean(out * out, axis=-1, keepdims=True)
        out = (out * lax.rsqrt(ms + EPS)) * nf_ref[...]
    o_ref[0] = out


def _mix(final, x, ya, yb, u, vv, sgate, ws, bs_exp, wout, nf):
    B, S, D = x.shape
    grid = (B, S // TM)
    row = lambda width: pl.BlockSpec((1, TM, width), lambda b, i: (b, i, 0))
    vec = lambda width: pl.BlockSpec((1, width), lambda b, i: (0, 0))
    return pl.pallas_call(
        functools.partial(_mix_kernel, final),
        grid=grid,
        in_specs=[
            row(D),
            row(D_CONV),
            row(D_ATTN),
            row(D_SGU),
            row(D_SGU),
            row(D_MIX - D_CONV),
            pl.BlockSpec((N_GROUPS, CHUNK, CHUNK), lambda b, i: (0, 0, 0)),
            pl.BlockSpec((CHUNK, D_SGU), lambda b, i: (0, 0)),
            pl.BlockSpec((D_MIX, D), lambda b, i: (0, 0)),
            vec(D),
        ],
        out_specs=row(D),
        out_shape=jax.ShapeDtypeStruct((B, S, D), F32),
        scratch_shapes=[
            pltpu.VMEM((TM, D_MIX), BF16),
        ],
        compiler_params=pltpu.CompilerParams(
            dimension_semantics=("arbitrary", "arbitrary"),
            vmem_limit_bytes=VMEM_LIMIT),
        name="mix_final" if final else "mix",
    )(x, ya, yb, u, vv, sgate, ws, bs_exp, wout, nf)


def kernel(x, norm_g, w_in, conv_w, conv_b, conv_ln_g, conv_ln_b, lam_q1, lam_k1, lam_q2, lam_k2,
           subln_g, sgu_ln_g, sgu_ln_b, w_s, b_s, w_out, norm_f):
    depth = w_in.shape[0]
    slopes = jnp.exp2(-8.0 * (jnp.arange(N_HEADS, dtype=F32) + 1.0) / N_HEADS)
    slopes = jnp.broadcast_to(slopes[:, None, None], (N_HEADS, 1, 128))
    pos = _key_positions()
    r2 = lambda a: a.reshape(1, -1)
    for l in range(depth):
        lam_init = 0.8 - 0.6 * math.exp(-0.3 * l)
        ya, q, k, vt, u, vv, sgate = _in_proj(
            x, r2(norm_g[l]), w_in[l].astype(BF16), r2(sgu_ln_g[l]), r2(sgu_ln_b[l]),
            conv_w[l], r2(conv_b[l]), r2(conv_ln_g[l]), r2(conv_ln_b[l]))
        yb = _attention(q, k, vt, pos, slopes, r2(lam_q1[l]), r2(lam_k1[l]), r2(lam_q2[l]),
                        r2(lam_k2[l]), r2(subln_g[l]), lam_init)
        bs_exp = jnp.repeat(b_s[l].T, GROUP_DIM, axis=1)
        x = _mix(l == depth - 1, x, ya, yb, u, vv, sgate, w_s[l], bs_exp,
                 w_out[l].astype(BF16), r2(norm_f))
    return x
```

```python
import functools
import math

import jax
import jax.numpy as jnp
from jax import lax
from jax.experimental import pallas as pl
from jax.experimental.pallas import tpu as pltpu

D_MODEL = 1024
D_MIX = D_MODEL
D_CONV = D_MIX // 4
D_ATTN = D_MIX // 2
D_SGU = D_MIX // 4
N_HEADS = 4
V_DIM = D_ATTN // N_HEADS
QK_DIM = V_DIM // 2
CONV_WIDTH = 31
N_GROUPS = 4
GROUP_DIM = D_SGU // N_GROUPS
CHUNK = 128
D_IN = 2 * D_CONV + 3 * D_ATTN + 2 * D_SGU + D_MIX
EPS = 1e-6
NEG = -1e30

C_CONV = 0
C_Q = C_CONV + 2 * D_CONV
C_K = C_Q + D_ATTN
C_V = C_K + D_ATTN
C_SGU = C_V + D_ATTN
C_GATE = C_SGU + 2 * D_SGU

TM = 512
TQ = 2048
TK = 512
STRIP = TK
Q_STRIPS = TQ // STRIP
N_STRIPS = 2 * Q_STRIPS
QK_AHEAD = 4
S_SLOTS = QK_AHEAD + 1
PV_K = 256
SUBLANES = 8
HALO = 32
CONV_ROWS = TM // 4
V_AUG = V_DIM + 16
N_COEF = 3
POS_SPLIT = 256
LOG2E = 1.0 / math.log(2.0)
VMEM_LIMIT = 56 * 1024 * 1024

F32 = jnp.float32
BF16 = jnp.bfloat16


def _layer_norm(v, g, b):
    mu = jnp.mean(v, axis=-1, keepdims=True)
    vc = v - mu
    var = jnp.mean(vc * vc, axis=-1, keepdims=True)
    return (vc * lax.rsqrt(var + EPS)) * g + b


def _silu(v):
    return v * jax.nn.sigmoid(v)


def _causal_conv(zbuf_ref, cw_ref, bias, r0, n):
    base = HALO - (CONV_WIDTH - 1)
    conv = jnp.zeros((n, D_CONV), F32) + bias
    for b in range(SUBLANES):
        rows = n if b == 0 else n + SUBLANES
        part = None
        for w in range(CONV_WIDTH):
            if (base + w) % SUBLANES != b:
                continue
            term = zbuf_ref[pl.ds(r0 + base + w - b, rows), :] * cw_ref[w:w + 1, :]
            part = term if part is None else part + term
        conv = conv + (part if b == 0 else part[b:b + n, :])
    return conv


def _zero_after(v):
    bits = pltpu.bitcast(v[:SUBLANES, -D_CONV:], jnp.uint32)
    return ((bits >> 16) >> 16)[:1, :].astype(F32)


def _proj_body(x, g_ref, w_ref, slg_ref, slb_ref, cw_ref, cb_ref, clg_ref, clb_ref,
               ya_ref, q_ref, k_ref, vt_ref, u_ref, vv_ref, sgate_ref, zbuf_ref, sgate_a_ref):
    ms = jnp.mean(x * x, axis=-1, keepdims=True)
    h = ((x * lax.rsqrt(ms + EPS)) * g_ref[...]).astype(BF16)

    def proj(lo, width):
        return jnp.dot(h, w_ref[:, lo:lo + width], preferred_element_type=F32)

    @pl.when(pl.program_id(1) == 0)
    def _():
        zbuf_ref[:HALO, :] = jnp.zeros((HALO, D_CONV), F32)

    def conv_chunk(c, after):
        r0, n = c * CONV_ROWS, CONV_ROWS
        conv = _causal_conv(zbuf_ref, cw_ref, cb_ref[...] + _zero_after(after), r0, n)
        y_a = _silu(_layer_norm(conv, clg_ref[...], clb_ref[...]))
        ya_ref[0, r0:r0 + n, :] = (y_a * sgate_a_ref[r0:r0 + n, :]).astype(BF16)

    pc = proj(C_CONV, 2 * D_CONV)
    zbuf_ref[HALO:, :] = pc[:, :D_CONV] * jax.nn.sigmoid(pc[:, D_CONV:])
    ga = proj(C_GATE, D_CONV)
    sgate_a_ref[...] = _silu(ga)
    conv_chunk(0, ga)
    qp = proj(C_Q, D_ATTN)
    q_ref[0] = (qp * (LOG2E / math.sqrt(QK_DIM))).astype(BF16)
    conv_chunk(1, qp)
    kp = proj(C_K, D_ATTN)
    k_ref[0] = kp.astype(BF16)
    conv_chunk(2, kp)
    v = proj(C_V, D_ATTN)
    for hd in range(N_HEADS):
        for c in range(TM // TK):
            blk = v[c * TK:(c + 1) * TK, hd * V_DIM:(hd + 1) * V_DIM]
            vt_ref[0, hd, c, :V_DIM, :] = blk.T.astype(BF16)
            vt_ref[0, hd, c, V_DIM:, :] = jnp.ones((V_AUG - V_DIM, TK), BF16)
    conv_chunk(3, v)
    ps = proj(C_SGU, 2 * D_SGU)
    u_ref[0] = ps[:, :D_SGU]
    vv_ref[0] = _layer_norm(ps[:, D_SGU:], slg_ref[...], slb_ref[...]).astype(BF16)
    sgate_ref[0] = _silu(proj(C_GATE + D_CONV, D_MIX - D_CONV)).astype(BF16)
    zbuf_ref[:HALO, :] = zbuf_ref[TM:, :]


def _in_proj_kernel(x_ref, *refs):
    _proj_body(x_ref[0], *refs)


def _row(width):
    return pl.BlockSpec((1, TM, width), lambda b, i: (b, i, 0))


def _vec(width):
    return pl.BlockSpec((1, width), lambda b, i: (0, 0))


def _proj_specs(B, S):
    D = D_MODEL
    n_k = S // TK
    in_specs = [
        _vec(D),
        pl.BlockSpec((D, D_IN), lambda b, i: (0, 0)),
        _vec(D_SGU), _vec(D_SGU),
        pl.BlockSpec((CONV_WIDTH, D_CONV), lambda b, i: (0, 0)),
        _vec(D_CONV), _vec(D_CONV), _vec(D_CONV),
    ]
    out_specs = [
        _row(D_CONV),
        _row(D_ATTN),
        _row(D_ATTN),
        pl.BlockSpec((1, N_HEADS, TM // TK, V_AUG, TK), lambda b, i: (b, 0, i, 0, 0)),
        _row(D_SGU),
        _row(D_SGU),
        _row(D_MIX - D_CONV),
    ]
    out_shape = [
        jax.ShapeDtypeStruct((B, S, D_CONV), BF16),
        jax.ShapeDtypeStruct((B, S, D_ATTN), BF16),
        jax.ShapeDtypeStruct((B, S, D_ATTN), BF16),
        jax.ShapeDtypeStruct((B, N_HEADS, n_k, V_AUG, TK), BF16),
        jax.ShapeDtypeStruct((B, S, D_SGU), F32),
        jax.ShapeDtypeStruct((B, S, D_SGU), BF16),
        jax.ShapeDtypeStruct((B, S, D_MIX - D_CONV), BF16),
    ]
    scratch = [
        pltpu.VMEM((TM + HALO, D_CONV), F32),
        pltpu.VMEM((TM, D_CONV), F32),
    ]
    return in_specs, out_specs, out_shape, scratch


def _in_proj(x, proj_params):
    B, S, D = x.shape
    in_specs, out_specs, out_shape, scratch = _proj_specs(B, S)
    return pl.pallas_call(
        _in_proj_kernel,
        grid=(B, S // TM),
        in_specs=[_row(D)] + in_specs,
        out_specs=out_specs,
        out_shape=out_shape,
        scratch_shapes=scratch,
        compiler_params=pltpu.CompilerParams(
            dimension_semantics=("arbitrary", "arbitrary"),
            vmem_limit_bytes=VMEM_LIMIT),
        name="in_proj",
    )(x, *proj_params)


def _attn_kernel(lam_init, q_ref, k_ref, vt_ref, pos_ref, slope_ref, lq1_ref, lk1_ref, lq2_ref,
                 lk2_ref, g_ref, o_ref, qt_ref, m_ref, acc_ref, st_ref):
    i = pl.program_id(2)
    coef = slope_ref[0][:, :1] * LOG2E

    qf = q_ref[0].astype(F32).T
    sub = lax.broadcasted_iota(jnp.int32, (V_DIM, TQ), 0)
    qt_ref[:V_DIM, :TQ] = jnp.where(sub < QK_DIM, qf, 0.0).astype(BF16)
    qt_ref[:V_DIM, TQ:] = jnp.where(sub >= QK_DIM, qf, 0.0).astype(BF16)

    c0 = coef.astype(BF16).astype(F32)
    c1 = (coef - c0).astype(BF16).astype(F32)
    c2 = (coef - c0 - c1).astype(BF16).astype(F32)
    row = lax.broadcasted_iota(jnp.int32, (V_DIM, 2 * TQ), 0)
    term = row % N_COEF
    terms = jnp.where(term == 0, c0, jnp.where(term == 1, c1, c2))
    qt_ref[V_DIM:, :] = jnp.where(row < 2 * N_COEF, terms, 0.0).astype(BF16)

    m_ref[...] = jnp.full(m_ref.shape, NEG, F32)
    acc_ref[...] = jnp.zeros(acc_ref.shape, F32)

    cols = lambda s: slice(s * STRIP, (s + 1) * STRIP)

    def key_tile(j):
        kb = k_ref[0, pl.ds(pl.multiple_of(j * TK, TK), TK), :]
        return jnp.concatenate([kb, pos_ref[...]], axis=1)

    def score(kaug, s):
        return jnp.dot(kaug, qt_ref[:, cols(s)], preferred_element_type=F32)

    def run(tiles):
        units = []
        for j, modes in tiles:
            kaug = key_tile(j)
            c = coef * (j * TK - i * TQ).astype(F32)
            units += [(j, kaug, c, s, modes[s]) for s in range(N_STRIPS) if modes[s]]

        def qk(u):
            st_ref[u % S_SLOTS] = score(units[u][1], units[u][3])

        def softmax_pv(u):
            j, _, c, s, mode = units[u]
            st = st_ref.at[u % S_SLOTS]
            if mode == "diag":
                kr = lax.broadcasted_iota(jnp.int32, (TK, STRIP), 0)
                qr = lax.broadcasted_iota(jnp.int32, (TK, STRIP), 1)
                st[...] = jnp.where(kr <= qr, st[...], NEG)
            m_old = m_ref[:, cols(s)]
            m_new = jnp.maximum(m_old, jnp.max(st[...], axis=0, keepdims=True) + c)
            alpha = jnp.exp2(m_old - m_new)
            shift = m_new - c
            o = None
            for h in range(TK // PV_K):
                keys = slice(h * PV_K, (h + 1) * PV_K)
                p = jnp.exp2(st[keys, :] - shift).astype(BF16)
                part = jnp.dot(vt_ref[0, 0, j, :, keys], p, preferred_element_type=F32)
                o = part if o is None else o + part
            acc_ref[:, cols(s)] = alpha * acc_ref[:, cols(s)] + o
            m_ref[:, cols(s)] = m_new

        for t in range(-QK_AHEAD, len(units)):
            if t + QK_AHEAD < len(units):
                qk(t + QK_AHEAD)
            if t >= 0:
                softmax_pv(t)

    def body(jj, carry):
        run([(jj * Q_STRIPS + d, ["full"] * N_STRIPS) for d in range(Q_STRIPS)])
        return carry

    lax.fori_loop(0, i, body, 0)
    run([(i * Q_STRIPS + d,
          [None if s % Q_STRIPS < d else "diag" if s % Q_STRIPS == d else "full"
           for s in range(N_STRIPS)]) for d in range(Q_STRIPS)])

    o = acc_ref[:V_DIM, :] / acc_ref[V_DIM:V_DIM + 1, :]
    lam = (jnp.exp(jnp.sum(lq1_ref[...] * lk1_ref[...], axis=-1, keepdims=True))
           - jnp.exp(jnp.sum(lq2_ref[...] * lk2_ref[...], axis=-1, keepdims=True))
           + lam_init)
    od = o[:, :TQ] - lam * o[:, TQ:]
    ms = jnp.mean(od * od, axis=0, keepdims=True)
    y = (od * lax.rsqrt(ms + EPS)).T
    o_ref[0] = ((y * g_ref[...]) * (1.0 - lam_init)).astype(BF16)


def _key_positions():
    jr = jnp.arange(TK, dtype=jnp.int32)
    lo = (jr % POS_SPLIT).astype(F32)
    hi = (jr - jr % POS_SPLIT).astype(F32)
    pos = jnp.stack([lo] * N_COEF + [hi] * N_COEF, axis=1)
    return jnp.pad(pos, ((0, 0), (0, V_DIM - 2 * N_COEF))).astype(BF16)


def _attention(q, k, vt, pos, slopes, lq1, lk1, lq2, lk2, subln_g, lam_init):
    B, S, _ = q.shape
    n_k = S // TK
    grid = (B, N_HEADS, S // TQ)
    small = lambda width: pl.BlockSpec((1, width), lambda b, h, i: (0, 0))
    return pl.pallas_call(
        functools.partial(_attn_kernel, lam_init),
        grid=grid,
        in_specs=[
            pl.BlockSpec((1, TQ, V_DIM), lambda b, h, i: (b, i, h)),
            pl.BlockSpec((1, S, V_DIM), lambda b, h, i: (b, 0, h)),
            pl.BlockSpec((1, 1, n_k, V_AUG, TK), lambda b, h, i: (b, h, 0, 0, 0)),
            pl.BlockSpec((TK, V_DIM), lambda b, h, i: (0, 0)),
            pl.BlockSpec((1, 1, 128), lambda b, h, i: (h, 0, 0)),
            small(QK_DIM), small(QK_DIM), small(QK_DIM), small(QK_DIM),
            small(V_DIM),
        ],
        out_specs=pl.BlockSpec((1, TQ, V_DIM), lambda b, h, i: (b, i, h)),
        out_shape=jax.ShapeDtypeStruct((B, S, D_ATTN), BF16),
        scratch_shapes=[
            pltpu.VMEM((2 * V_DIM, 2 * TQ), BF16),
            pltpu.VMEM((1, 2 * TQ), F32),
            pltpu.VMEM((V_AUG, 2 * TQ), F32),
            pltpu.VMEM((S_SLOTS, TK, STRIP), F32),
        ],
        compiler_params=pltpu.CompilerParams(
            dimension_semantics=("arbitrary", "arbitrary", "arbitrary"),
            vmem_limit_bytes=VMEM_LIMIT),
        name="diff_attn",
    )(q, k, vt, pos, slopes, lq1, lk1, lq2, lk2, subln_g)


def _mix_body(x_ref, ya_ref, yb_ref, u_ref, vv_ref, sgate_ref, ws_ref, bs_ref, wout_ref,
              ybuf_ref):
    ybuf_ref[:, :D_CONV] = ya_ref[0]

    ybuf_ref[:, D_CONV:D_CONV + D_ATTN] = yb_ref[0] * sgate_ref[0, :, :D_ATTN]

    tr = lax.broadcasted_iota(jnp.int32, (CHUNK, CHUNK), 0)
    tc = lax.broadcasted_iota(jnp.int32, (CHUNK, CHUNK), 1)
    wm = [jnp.where(tr >= tc, ws_ref[g], 0.0).astype(BF16) for g in range(N_GROUPS)]
    lane_group = lax.broadcasted_iota(jnp.int32, (CHUNK, D_SGU), 1) // GROUP_DIM
    for c in range(TM // CHUNK):
        rows = slice(c * CHUNK, (c + 1) * CHUNK)
        vc = vv_ref[0, rows, :]
        mixed = jnp.zeros((CHUNK, D_SGU), F32)
        for g in range(N_GROUPS):
            r = jnp.dot(wm[g], vc, preferred_element_type=F32)
            mixed = jnp.where(lane_group == g, r, mixed)
        y_c = u_ref[0, rows, :] * (mixed + bs_ref[...])
        ybuf_ref[rows, D_CONV + D_ATTN:] = (
            y_c * sgate_ref[0, rows, D_ATTN:].astype(F32)).astype(BF16)

    return x_ref[0] + jnp.dot(ybuf_ref[...], wout_ref[...], preferred_element_type=F32)


N_MIX_IN = 9
N_PROJ_IN = 8


def _mix_final_kernel(*refs):
    mix_in, (nf_ref, o_ref, ybuf_ref) = refs[:N_MIX_IN], refs[N_MIX_IN:]
    out = _mix_body(*mix_in, ybuf_ref)
    ms = jnp.mean(out * out, axis=-1, keepdims=True)
    o_ref[0] = (out * lax.rsqrt(ms + EPS)) * nf_ref[...]


def _mix_proj_kernel(*refs):
    mix_in = refs[:N_MIX_IN]
    proj_in = refs[N_MIX_IN:N_MIX_IN + N_PROJ_IN]
    xo_ref, *proj_out = refs[N_MIX_IN + N_PROJ_IN:-3]
    ybuf_ref, zbuf_ref, sgate_a_ref = refs[-3:]
    out = _mix_body(*mix_in, ybuf_ref)
    xo_ref[0] = out
    _proj_body(out, *proj_in, *proj_out, zbuf_ref, sgate_a_ref)


def _mix_specs():
    D = D_MODEL
    in_specs = [
        _row(D),
        _row(D_CONV),
        _row(D_ATTN),
        _row(D_SGU),
        _row(D_SGU),
        _row(D_MIX - D_CONV),
        pl.BlockSpec((N_GROUPS, CHUNK, CHUNK), lambda b, i: (0, 0, 0)),
        pl.BlockSpec((CHUNK, D_SGU), lambda b, i: (0, 0)),
        pl.BlockSpec((D_MIX, D), lambda b, i: (0, 0)),
    ]
    scratch = [pltpu.VMEM((TM, D_MIX), BF16)]
    return in_specs, scratch


def _mix_final(mix_args, nf):
    B, S, D = mix_args[0].shape
    in_specs, scratch = _mix_specs()
    return pl.pallas_call(
        _mix_final_kernel,
        grid=(B, S // TM),
        in_specs=in_specs + [_vec(D)],
        out_specs=_row(D),
        out_shape=jax.ShapeDtypeStruct((B, S, D), F32),
        scratch_shapes=scratch,
        compiler_params=pltpu.CompilerParams(
            dimension_semantics=("arbitrary", "arbitrary"),
            vmem_limit_bytes=VMEM_LIMIT),
        name="mix_final",
    )(*mix_args, nf)


def _mix_proj(mix_args, proj_params):
    B, S, D = mix_args[0].shape
    mix_in, mix_scratch = _mix_specs()
    proj_in, proj_out, proj_shape, proj_scratch = _proj_specs(B, S)
    outs = pl.pallas_call(
        _mix_proj_kernel,
        grid=(B, S // TM),
        in_specs=mix_in + proj_in,
        out_specs=[_row(D)] + proj_out,
        out_shape=[jax.ShapeDtypeStruct((B, S, D), F32)] + proj_shape,
        scratch_shapes=mix_scratch + proj_scratch,
        compiler_params=pltpu.CompilerParams(
            dimension_semantics=("arbitrary", "arbitrary"),
            vmem_limit_bytes=VMEM_LIMIT),
        name="mix_proj",
    )(*mix_args, *proj_params)
    return outs[0], outs[1:]


def kernel(x, norm_g, w_in, conv_w, conv_b, conv_ln_g, conv_ln_b, lam_q1, lam_k1, lam_q2, lam_k2,
           subln_g, sgu_ln_g, sgu_ln_b, w_s, b_s, w_out, norm_f):
    depth = w_in.shape[0]
    slopes = jnp.exp2(-8.0 * (jnp.arange(N_HEADS, dtype=F32) + 1.0) / N_HEADS)
    slopes = jnp.broadcast_to(slopes[:, None, None], (N_HEADS, 1, 128))
    pos = _key_positions()
    r2 = lambda a: a.reshape(1, -1)

    def proj_params(l):
        return (r2(norm_g[l]), w_in[l].astype(BF16), r2(sgu_ln_g[l]), r2(sgu_ln_b[l]),
                conv_w[l], r2(conv_b[l]), r2(conv_ln_g[l]), r2(conv_ln_b[l]))

    ya, q, k, vt, u, vv, sgate = _in_proj(x, proj_params(0))
    for l in range(depth):
        lam_init = 0.8 - 0.6 * math.exp(-0.3 * l)
        yb = _attention(q, k, vt, pos, slopes, r2(lam_q1[l]), r2(lam_k1[l]), r2(lam_q2[l]),
                        r2(lam_k2[l]), r2(subln_g[l]), lam_init)
        bs_exp = jnp.repeat(b_s[l].T, GROUP_DIM, axis=1)
        mix_args = (x, ya, yb, u, vv, sgate, w_s[l], bs_exp, w_out[l].astype(BF16))
        if l == depth - 1:
            return _mix_final(mix_args, r2(norm_f))
        x, (ya, q, k, vt, u, vv, sgate) = _mix_proj(mix_args, proj_params(l + 1))
```

```python
import functools
import math

import jax
import jax.numpy as jnp
from jax import lax
from jax.experimental import pallas as pl
from jax.experimental.pallas import tpu as pltpu

D_MODEL = 1024
D_MIX = D_MODEL
D_CONV = D_MIX // 4
D_ATTN = D_MIX // 2
D_SGU = D_MIX // 4
N_HEADS = 4
V_DIM = D_ATTN // N_HEADS
QK_DIM = V_DIM // 2
CONV_WIDTH = 31
N_GROUPS = 4
GROUP_DIM = D_SGU // N_GROUPS
CHUNK = 128
D_IN = 2 * D_CONV + 3 * D_ATTN + 2 * D_SGU + D_MIX
EPS = 1e-6
NEG = -1e30

C_CONV = 0
C_Q = C_CONV + 2 * D_CONV
C_K = C_Q + D_ATTN
C_V = C_K + D_ATTN
C_SGU = C_V + D_ATTN
C_GATE = C_SGU + 2 * D_SGU

TM = 512
TQ = 2048
TK = 512
STRIP = TK
Q_STRIPS = TQ // STRIP
N_STRIPS = 2 * Q_STRIPS
QK_AHEAD = 4
S_SLOTS = QK_AHEAD + 1
HALF = TK // 2
SUBLANES = 8
HALO = 32
CONV_ROWS = TM // 4
V_AUG = V_DIM + 16
N_COEF = 3
POS_SPLIT = 256
LOG2E = 1.0 / math.log(2.0)
VMEM_LIMIT = 56 * 1024 * 1024

F32 = jnp.float32
BF16 = jnp.bfloat16


def _layer_norm(v, g, b):
    mu = jnp.mean(v, axis=-1, keepdims=True)
    vc = v - mu
    var = jnp.mean(vc * vc, axis=-1, keepdims=True)
    return (vc * lax.rsqrt(var + EPS)) * g + b


def _silu(v):
    return v * jax.nn.sigmoid(v)


def _causal_conv(zbuf_ref, cw_ref, bias, r0, n):
    base = HALO - (CONV_WIDTH - 1)
    conv = jnp.zeros((n, D_CONV), F32) + bias
    for b in range(SUBLANES):
        rows = n if b == 0 else n + SUBLANES
        part = None
        for w in range(CONV_WIDTH):
            if (base + w) % SUBLANES != b:
                continue
            term = zbuf_ref[pl.ds(r0 + base + w - b, rows), :] * cw_ref[w:w + 1, :]
            part = term if part is None else part + term
        conv = conv + (part if b == 0 else part[b:b + n, :])
    return conv


def _zero_after(v):
    bits = pltpu.bitcast(v[:SUBLANES, -D_CONV:], jnp.uint32)
    return ((bits >> 16) >> 16)[:1, :].astype(F32)


def _proj_body(x, g_ref, w_ref, slg_ref, slb_ref, cw_ref, cb_ref, clg_ref, clb_ref,
               ya_ref, q_ref, k_ref, vt_ref, u_ref, vv_ref, sgate_ref, zbuf_ref, sgate_a_ref):
    ms = jnp.mean(x * x, axis=-1, keepdims=True)
    h = ((x * lax.rsqrt(ms + EPS)) * g_ref[...]).astype(BF16)

    def proj(lo, width):
        return jnp.dot(h, w_ref[:, lo:lo + width], preferred_element_type=F32)

    @pl.when(pl.program_id(1) == 0)
    def _():
        zbuf_ref[:HALO, :] = jnp.zeros((HALO, D_CONV), F32)

    def conv_chunk(c, after):
        r0, n = c * CONV_ROWS, CONV_ROWS
        conv = _causal_conv(zbuf_ref, cw_ref, cb_ref[...] + _zero_after(after), r0, n)
        y_a = _silu(_layer_norm(conv, clg_ref[...], clb_ref[...]))
        ya_ref[0, r0:r0 + n, :] = (y_a * sgate_a_ref[r0:r0 + n, :]).astype(BF16)

    pc = proj(C_CONV, 2 * D_CONV)
    zbuf_ref[HALO:, :] = pc[:, :D_CONV] * jax.nn.sigmoid(pc[:, D_CONV:])
    ga = proj(C_GATE, D_CONV)
    sgate_a_ref[...] = _silu(ga)
    conv_chunk(0, ga)
    qp = proj(C_Q, D_ATTN)
    for hd in range(N_HEADS):
        qh = qp[:, hd * V_DIM:(hd + 1) * V_DIM] * (LOG2E / math.sqrt(QK_DIM))
        q_ref[0, hd] = qh.T.astype(BF16)
    conv_chunk(1, qp)
    kp = proj(C_K, D_ATTN)
    k_ref[0] = kp.astype(BF16)
    conv_chunk(2, kp)
    v = proj(C_V, D_ATTN)
    for hd in range(N_HEADS):
        for c in range(TM // TK):
            blk = v[c * TK:(c + 1) * TK, hd * V_DIM:(hd + 1) * V_DIM]
            vt_ref[0, hd, c, :V_DIM, :] = blk.T.astype(BF16)
            vt_ref[0, hd, c, V_DIM:, :] = jnp.ones((V_AUG - V_DIM, TK), BF16)
    conv_chunk(3, v)
    ps = proj(C_SGU, 2 * D_SGU)
    u_ref[0] = ps[:, :D_SGU]
    vv_ref[0] = _layer_norm(ps[:, D_SGU:], slg_ref[...], slb_ref[...]).astype(BF16)
    sgate_ref[0] = _silu(proj(C_GATE + D_CONV, D_MIX - D_CONV)).astype(BF16)
    zbuf_ref[:HALO, :] = zbuf_ref[TM:, :]


def _in_proj_kernel(x_ref, *refs):
    _proj_body(x_ref[0], *refs)


def _row(width):
    return pl.BlockSpec((1, TM, width), lambda b, i: (b, i, 0))


def _vec(width):
    return pl.BlockSpec((1, width), lambda b, i: (0, 0))


def _proj_specs(B, S):
    D = D_MODEL
    n_k = S // TK
    in_specs = [
        _vec(D),
        pl.BlockSpec((D, D_IN), lambda b, i: (0, 0)),
        _vec(D_SGU), _vec(D_SGU),
        pl.BlockSpec((CONV_WIDTH, D_CONV), lambda b, i: (0, 0)),
        _vec(D_CONV), _vec(D_CONV), _vec(D_CONV),
    ]
    out_specs = [
        _row(D_CONV),
        pl.BlockSpec((1, N_HEADS, V_DIM, TM), lambda b, i: (b, 0, 0, i)),
        _row(D_ATTN),
        pl.BlockSpec((1, N_HEADS, TM // TK, V_AUG, TK), lambda b, i: (b, 0, i, 0, 0)),
        _row(D_SGU),
        _row(D_SGU),
        _row(D_MIX - D_CONV),
    ]
    out_shape = [
        jax.ShapeDtypeStruct((B, S, D_CONV), BF16),
        jax.ShapeDtypeStruct((B, N_HEADS, V_DIM, S), BF16),
        jax.ShapeDtypeStruct((B, S, D_ATTN), BF16),
        jax.ShapeDtypeStruct((B, N_HEADS, n_k, V_AUG, TK), BF16),
        jax.ShapeDtypeStruct((B, S, D_SGU), F32),
        jax.ShapeDtypeStruct((B, S, D_SGU), BF16),
        jax.ShapeDtypeStruct((B, S, D_MIX - D_CONV), BF16),
    ]
    scratch = [
        pltpu.VMEM((TM + HALO, D_CONV), F32),
        pltpu.VMEM((TM, D_CONV), F32),
    ]
    return in_specs, out_specs, out_shape, scratch


def _in_proj(x, proj_params):
    B, S, D = x.shape
    in_specs, out_specs, out_shape, scratch = _proj_specs(B, S)
    return pl.pallas_call(
        _in_proj_kernel,
        grid=(B, S // TM),
        in_specs=[_row(D)] + in_specs,
        out_specs=out_specs,
        out_shape=out_shape,
        scratch_shapes=scratch,
        compiler_params=pltpu.CompilerParams(
            dimension_semantics=("arbitrary", "arbitrary"),
            vmem_limit_bytes=VMEM_LIMIT),
        name="in_proj",
    )(x, *proj_params)


def _attn_kernel(lam_init, q_ref, k_ref, vt_ref, pos_ref, slope_ref, lq1_ref, lk1_ref, lq2_ref,
                 lk2_ref, g_ref, o_ref, qt_ref, m_ref, acc_ref, st_ref):
    i = pl.program_id(2)
    coef = slope_ref[0][:, :1] * LOG2E

    zeros = jnp.zeros((QK_DIM, TQ), BF16)
    qt_ref[:QK_DIM, :TQ] = q_ref[0, 0, :QK_DIM, :]
    qt_ref[QK_DIM:V_DIM, :TQ] = zeros
    qt_ref[:QK_DIM, TQ:] = zeros
    qt_ref[QK_DIM:V_DIM, TQ:] = q_ref[0, 0, QK_DIM:, :]

    c0 = coef.astype(BF16).astype(F32)
    c1 = (coef - c0).astype(BF16).astype(F32)
    c2 = (coef - c0 - c1).astype(BF16).astype(F32)
    row = lax.broadcasted_iota(jnp.int32, (V_DIM, 2 * TQ), 0)
    term = row % N_COEF
    terms = jnp.where(term == 0, c0, jnp.where(term == 1, c1, c2))
    qt_ref[V_DIM:, :] = jnp.where(row < 2 * N_COEF, terms, 0.0).astype(BF16)

    m_ref[...] = jnp.full(m_ref.shape, NEG, F32)
    acc_ref[...] = jnp.zeros(acc_ref.shape, F32)

    cols = lambda s: slice(s * STRIP, (s + 1) * STRIP)

    def key_tile(j):
        kb = k_ref[0, pl.ds(pl.multiple_of(j * TK, TK), TK), :]
        return jnp.concatenate([kb, pos_ref[...]], axis=1)

    def score(kaug, columns):
        return jnp.dot(kaug, qt_ref[:, columns], preferred_element_type=F32)

    def run(tiles):
        units = []
        for j, modes in tiles:
            kaug = key_tile(j)
            c = coef * (j * TK - i * TQ).astype(F32)
            units += [(j, kaug, c, s, modes[s]) for s in range(N_STRIPS) if modes[s]]

        def qk(u):
            _, kaug, _, s, mode = units[u]
            st = st_ref.at[u % S_SLOTS]
            if mode == "diag":
                right = slice(s * STRIP + HALF, (s + 1) * STRIP)
                st[:HALF, :] = score(kaug[:HALF], cols(s))
                st[HALF:, HALF:] = score(kaug[HALF:], right)
            else:
                st[...] = score(kaug, cols(s))

        def softmax_pv(u):
            j, _, c, s, mode = units[u]
            st = st_ref.at[u % S_SLOTS]
            m_old = m_ref[:, cols(s)]
            if mode == "diag":
                kr = lax.broadcasted_iota(jnp.int32, (HALF, HALF), 0)
                qr = lax.broadcasted_iota(jnp.int32, (HALF, HALF), 1)
                top = jnp.concatenate(
                    [jnp.where(kr <= qr, st[:HALF, :HALF], NEG), st[:HALF, HALF:]], axis=1)
                bot = jnp.where(kr <= qr, st[HALF:, HALF:], NEG)
                tmax = jnp.max(top, axis=0, keepdims=True)
                bmax = jnp.max(bot, axis=0, keepdims=True)
                blk_max = jnp.concatenate(
                    [tmax[:, :HALF], jnp.maximum(tmax[:, HALF:], bmax)], axis=1)
            else:
                blk_max = jnp.max(st[...], axis=0, keepdims=True)
            m_new = jnp.maximum(m_old, blk_max + c)
            alpha = jnp.exp2(m_old - m_new)
            shift = m_new - c
            if mode == "diag":
                o = jnp.dot(vt_ref[0, 0, j, :, :HALF], jnp.exp2(top - shift).astype(BF16),
                            preferred_element_type=F32)
                o_right = jnp.dot(vt_ref[0, 0, j, :, HALF:],
                                  jnp.exp2(bot - shift[:, HALF:]).astype(BF16),
                                  preferred_element_type=F32)
                o = jnp.concatenate([o[:, :HALF], o[:, HALF:] + o_right], axis=1)
            else:
                o = None
                for h in range(TK // HALF):
                    keys = slice(h * HALF, (h + 1) * HALF)
                    p = jnp.exp2(st[keys, :] - shift).astype(BF16)
                    part = jnp.dot(vt_ref[0, 0, j, :, keys], p, preferred_element_type=F32)
                    o = part if o is None else o + part
            acc_ref[:, cols(s)] = alpha * acc_ref[:, cols(s)] + o
            m_ref[:, cols(s)] = m_new

        for t in range(-QK_AHEAD, len(units)):
            if t + QK_AHEAD < len(units):
                qk(t + QK_AHEAD)
            if t >= 0:
                softmax_pv(t)

    def body(jj, carry):
        run([(jj * Q_STRIPS + d, ["full"] * N_STRIPS) for d in range(Q_STRIPS)])
        return carry

    lax.fori_loop(0, i, body, 0)
    run([(i * Q_STRIPS + d,
          [None if s % Q_STRIPS < d else "diag" if s % Q_STRIPS == d else "full"
           for s in range(N_STRIPS)]) for d in range(Q_STRIPS)])

    lam = (jnp.exp(jnp.sum(lq1_ref[...] * lk1_ref[...], axis=-1, keepdims=True))
           - jnp.exp(jnp.sum(lq2_ref[...] * lk2_ref[...], axis=-1, keepdims=True))
           + lam_init)
    inv = 1.0 / acc_ref[V_DIM:V_DIM + 1, :]
    od = (acc_ref[:V_DIM, :TQ] * inv[:, :TQ]
          - acc_ref[:V_DIM, TQ:] * (lam * inv[:, TQ:]))
    ms = jnp.mean(od * od, axis=0, keepdims=True)
    y = (od * lax.rsqrt(ms + EPS)).T
    o_ref[0] = ((y * g_ref[...]) * (1.0 - lam_init)).astype(BF16)


def _key_positions():
    jr = jnp.arange(TK, dtype=jnp.int32)
    lo = (jr % POS_SPLIT).astype(F32)
    hi = (jr - jr % POS_SPLIT).astype(F32)
    pos = jnp.stack([lo] * N_COEF + [hi] * N_COEF, axis=1)
    return jnp.pad(pos, ((0, 0), (0, V_DIM - 2 * N_COEF))).astype(BF16)


def _attention(q, k, vt, pos, slopes, lq1, lk1, lq2, lk2, subln_g, lam_init):
    B, S, _ = k.shape
    n_k = S // TK
    grid = (B, N_HEADS, S // TQ)
    small = lambda width: pl.BlockSpec((1, width), lambda b, h, i: (0, 0))
    return pl.pallas_call(
        functools.partial(_attn_kernel, lam_init),
        grid=grid,
        in_specs=[
            pl.BlockSpec((1, 1, V_DIM, TQ), lambda b, h, i: (b, h, 0, i)),
            pl.BlockSpec((1, S, V_DIM), lambda b, h, i: (b, 0, h)),
            pl.BlockSpec((1, 1, n_k, V_AUG, TK), lambda b, h, i: (b, h, 0, 0, 0)),
            pl.BlockSpec((TK, V_DIM), lambda b, h, i: (0, 0)),
            pl.BlockSpec((1, 1, 128), lambda b, h, i: (h, 0, 0)),
            small(QK_DIM), small(QK_DIM), small(QK_DIM), small(QK_DIM),
            small(V_DIM),
        ],
        out_specs=pl.BlockSpec((1, TQ, V_DIM), lambda b, h, i: (b, i, h)),
        out_shape=jax.ShapeDtypeStruct((B, S, D_ATTN), BF16),
        scratch_shapes=[
            pltpu.VMEM((2 * V_DIM, 2 * TQ), BF16),
            pltpu.VMEM((1, 2 * TQ), F32),
            pltpu.VMEM((V_AUG, 2 * TQ), F32),
            pltpu.VMEM((S_SLOTS, TK, STRIP), F32),
        ],
        compiler_params=pltpu.CompilerParams(
            dimension_semantics=("arbitrary", "arbitrary", "arbitrary"),
            vmem_limit_bytes=VMEM_LIMIT),
        name="diff_attn",
    )(q, k, vt, pos, slopes, lq1, lk1, lq2, lk2, subln_g)


def _mix_body(x_ref, ya_ref, yb_ref, u_ref, vv_ref, sgate_ref, ws_ref, bs_ref, wout_ref,
              ybuf_ref):
    ybuf_ref[:, :D_CONV] = ya_ref[0]

    ybuf_ref[:, D_CONV:D_CONV + D_ATTN] = yb_ref[0] * sgate_ref[0, :, :D_ATTN]

    tr = lax.broadcasted_iota(jnp.int32, (CHUNK, CHUNK), 0)
    tc = lax.broadcasted_iota(jnp.int32, (CHUNK, CHUNK), 1)
    wm = [jnp.where(tr >= tc, ws_ref[g], 0.0).astype(BF16) for g in range(N_GROUPS)]
    lane_group = lax.broadcasted_iota(jnp.int32, (CHUNK, D_SGU), 1) // GROUP_DIM
    for c in range(TM // CHUNK):
        rows = slice(c * CHUNK, (c + 1) * CHUNK)
        vc = vv_ref[0, rows, :]
        mixed = jnp.zeros((CHUNK, D_SGU), F32)
        for g in range(N_GROUPS):
            r = jnp.dot(wm[g], vc, preferred_element_type=F32)
            mixed = jnp.where(lane_group == g, r, mixed)
        y_c = u_ref[0, rows, :] * (mixed + bs_ref[...])
        ybuf_ref[rows, D_CONV + D_ATTN:] = (
            y_c * sgate_ref[0, rows, D_ATTN:].astype(F32)).astype(BF16)

    return x_ref[0] + jnp.dot(ybuf_ref[...], wout_ref[...], preferred_element_type=F32)


N_MIX_IN = 9
N_PROJ_IN = 8


def _mix_final_kernel(*refs):
    mix_in, (nf_ref, o_ref, ybuf_ref) = refs[:N_MIX_IN], refs[N_MIX_IN:]
    out = _mix_body(*mix_in, ybuf_ref)
    ms = jnp.mean(out * out, axis=-1, keepdims=True)
    o_ref[0] = (out * lax.rsqrt(ms + EPS)) * nf_ref[...]


def _mix_proj_kernel(*refs):
    mix_in = refs[:N_MIX_IN]
    proj_in = refs[N_MIX_IN:N_MIX_IN + N_PROJ_IN]
    xo_ref, *proj_out = refs[N_MIX_IN + N_PROJ_IN:-3]
    ybuf_ref, zbuf_ref, sgate_a_ref = refs[-3:]
    out = _mix_body(*mix_in, ybuf_ref)
    xo_ref[0] = out
    _proj_body(out, *proj_in, *proj_out, zbuf_ref, sgate_a_ref)


def _mix_specs():
    D = D_MODEL
    in_specs = [
        _row(D),
        _row(D_CONV),
        _row(D_ATTN),
        _row(D_SGU),
        _row(D_SGU),
        _row(D_MIX - D_CONV),
        pl.BlockSpec((N_GROUPS, CHUNK, CHUNK), lambda b, i: (0, 0, 0)),
        pl.BlockSpec((CHUNK, D_SGU), lambda b, i: (0, 0)),
        pl.BlockSpec((D_MIX, D), lambda b, i: (0, 0)),
    ]
    scratch = [pltpu.VMEM((TM, D_MIX), BF16)]
    return in_specs, scratch


def _mix_final(mix_args, nf):
    B, S, D = mix_args[0].shape
    in_specs, scratch = _mix_specs()
    return pl.pallas_call(
        _mix_final_kernel,
        grid=(B, S // TM),
        in_specs=in_specs + [_vec(D)],
        out_specs=_row(D),
        out_shape=jax.ShapeDtypeStruct((B, S, D), F32),
        scratch_shapes=scratch,
        compiler_params=pltpu.CompilerParams(
            dimension_semantics=("arbitrary", "arbitrary"),
            vmem_limit_bytes=VMEM_LIMIT),
        name="mix_final",
    )(*mix_args, nf)


def _mix_proj(mix_args, proj_params):
    B, S, D = mix_args[0].shape
    mix_in, mix_scratch = _mix_specs()
    proj_in, proj_out, proj_shape, proj_scratch = _proj_specs(B, S)
    outs = pl.pallas_call(
        _mix_proj_kernel,
        grid=(B, S // TM),
        in_specs=mix_in + proj_in,
        out_specs=[_row(D)] + proj_out,
        out_shape=[jax.ShapeDtypeStruct((B, S, D), F32)] + proj_shape,
        scratch_shapes=mix_scratch + proj_scratch,
        compiler_params=pltpu.CompilerParams(
            dimension_semantics=("arbitrary", "arbitrary"),
            vmem_limit_bytes=VMEM_LIMIT),
        name="mix_proj",
    )(*mix_args, *proj_params)
    return outs[0], outs[1:]


def kernel(x, norm_g, w_in, conv_w, conv_b, conv_ln_g, conv_ln_b, lam_q1, lam_k1, lam_q2, lam_k2,
           subln_g, sgu_ln_g, sgu_ln_b, w_s, b_s, w_out, norm_f):
    depth = w_in.shape[0]
    slopes = jnp.exp2(-8.0 * (jnp.arange(N_HEADS, dtype=F32) + 1.0) / N_HEADS)
    slopes = jnp.broadcast_to(slopes[:, None, None], (N_HEADS, 1, 128))
    pos = _key_positions()
    r2 = lambda a: a.reshape(1, -1)

    def proj_params(l):
        return (r2(norm_g[l]), w_in[l].astype(BF16), r2(sgu_ln_g[l]), r2(sgu_ln_b[l]),
                conv_w[l], r2(conv_b[l]), r2(conv_ln_g[l]), r2(conv_ln_b[l]))

    ya, q, k, vt, u, vv, sgate = _in_proj(x, proj_params(0))
    for l in range(depth):
        lam_init = 0.8 - 0.6 * math.exp(-0.3 * l)
        yb = _attention(q, k, vt, pos, slopes, r2(lam_q1[l]), r2(lam_k1[l]), r2(lam_q2[l]),
                        r2(lam_k2[l]), r2(subln_g[l]), lam_init)
        bs_exp = jnp.repeat(b_s[l].T, GROUP_DIM, axis=1)
        mix_args = (x, ya, yb, u, vv, sgate, w_s[l], bs_exp, w_out[l].astype(BF16))
        if l == depth - 1:
            return _mix_final(mix_args, r2(norm_f))
        x, (ya, q, k, vt, u, vv, sgate) = _mix_proj(mix_args, proj_params(l + 1))
```

```python
import functools
import math

import jax
import jax.numpy as jnp
from jax import lax
from jax.experimental import pallas as pl
from jax.experimental.pallas import tpu as pltpu

D_MODEL = 1024
D_MIX = D_MODEL
D_CONV = D_MIX // 4
D_ATTN = D_MIX // 2
D_SGU = D_MIX // 4
N_HEADS = 4
V_DIM = D_ATTN // N_HEADS
QK_DIM = V_DIM // 2
CONV_WIDTH = 31
N_GROUPS = 4
GROUP_DIM = D_SGU // N_GROUPS
CHUNK = 128
D_IN = 2 * D_CONV + 3 * D_ATTN + 2 * D_SGU + D_MIX
EPS = 1e-6
NEG = -1e30

C_CONV = 0
C_Q = C_CONV + 2 * D_CONV
C_K = C_Q + D_ATTN
C_V = C_K + D_ATTN
C_SGU = C_V + D_ATTN
C_GATE = C_SGU + 2 * D_SGU

TM = 512
TQ = 2048
TK = 512
STRIP = TK
Q_STRIPS = TQ // STRIP
N_STRIPS = 2 * Q_STRIPS
QK_AHEAD = 4
S_SLOTS = QK_AHEAD + 1
HALF = TK // 2
SUBLANES = 8
HALO = 32
CONV_ROWS = TM // 4
V_AUG = V_DIM + 16
N_COEF = 3
POS_SPLIT = 256
LOG2E = 1.0 / math.log(2.0)
VMEM_LIMIT = 56 * 1024 * 1024

F32 = jnp.float32
BF16 = jnp.bfloat16


def _layer_norm(v, g, b):
    mu = jnp.mean(v, axis=-1, keepdims=True)
    vc = v - mu
    var = jnp.mean(vc * vc, axis=-1, keepdims=True)
    return (vc * lax.rsqrt(var + EPS)) * g + b


def _silu(v):
    return v * jax.nn.sigmoid(v)


def _causal_conv(zbuf_ref, cw_ref, bias, r0, n):
    base = HALO - (CONV_WIDTH - 1)
    conv = jnp.zeros((n, D_CONV), F32) + bias
    for b in range(SUBLANES):
        rows = n if b == 0 else n + SUBLANES
        part = None
        for w in range(CONV_WIDTH):
            if (base + w) % SUBLANES != b:
                continue
            term = zbuf_ref[pl.ds(r0 + base + w - b, rows), :] * cw_ref[w:w + 1, :]
            part = term if part is None else part + term
        conv = conv + (part if b == 0 else part[b:b + n, :])
    return conv


def _zero_after(v):
    bits = pltpu.bitcast(v[:SUBLANES, -D_CONV:], jnp.uint32)
    return ((bits >> 16) >> 16)[:1, :].astype(F32)


def _proj_body(x, g_ref, w_ref, slg_ref, slb_ref, cw_ref, cb_ref, clg_ref, clb_ref,
               ya_ref, q_ref, k_ref, vt_ref, u_ref, vv_ref, sgate_ref, zbuf_ref, sgate_a_ref):
    ms = jnp.mean(x * x, axis=-1, keepdims=True)
    h = ((x * lax.rsqrt(ms + EPS)) * g_ref[...]).astype(BF16)

    def proj(lo, width):
        return jnp.dot(h, w_ref[:, lo:lo + width], preferred_element_type=F32)

    @pl.when(pl.program_id(1) == 0)
    def _():
        zbuf_ref[:HALO, :] = jnp.zeros((HALO, D_CONV), F32)

    def conv_chunk(c, after):
        r0, n = c * CONV_ROWS, CONV_ROWS
        conv = _causal_conv(zbuf_ref, cw_ref, cb_ref[...] + _zero_after(after), r0, n)
        y_a = _silu(_layer_norm(conv, clg_ref[...], clb_ref[...]))
        ya_ref[0, r0:r0 + n, :] = (y_a * sgate_a_ref[r0:r0 + n, :]).astype(BF16)

    pc = proj(C_CONV, 2 * D_CONV)
    zbuf_ref[HALO:, :] = pc[:, :D_CONV] * jax.nn.sigmoid(pc[:, D_CONV:])
    ga = proj(C_GATE, D_CONV)
    sgate_a_ref[...] = _silu(ga)
    conv_chunk(0, ga)
    qp = proj(C_Q, D_ATTN)
    for hd in range(N_HEADS):
        qh = qp[:, hd * V_DIM:(hd + 1) * V_DIM] * (LOG2E / math.sqrt(QK_DIM))
        q_ref[0, hd] = qh.T.astype(BF16)
    conv_chunk(1, qp)
    kp = proj(C_K, D_ATTN)
    k_ref[0] = kp.astype(BF16)
    conv_chunk(2, kp)
    v = proj(C_V, D_ATTN)
    for hd in range(N_HEADS):
        for c in range(TM // TK):
            blk = v[c * TK:(c + 1) * TK, hd * V_DIM:(hd + 1) * V_DIM]
            vt_ref[0, hd, c, :V_DIM, :] = blk.T.astype(BF16)
            vt_ref[0, hd, c, V_DIM:, :] = jnp.ones((V_AUG - V_DIM, TK), BF16)
    conv_chunk(3, v)
    ps = proj(C_SGU, 2 * D_SGU)
    u_ref[0] = ps[:, :D_SGU]
    vv_ref[0] = _layer_norm(ps[:, D_SGU:], slg_ref[...], slb_ref[...]).astype(BF16)
    sgate_ref[0] = _silu(proj(C_GATE + D_CONV, D_MIX - D_CONV)).astype(BF16)
    zbuf_ref[:HALO, :] = zbuf_ref[TM:, :]


def _in_proj_kernel(x_ref, *refs):
    _proj_body(x_ref[0], *refs)


def _row(width):
    return pl.BlockSpec((1, TM, width), lambda b, i: (b, i, 0))


def _layer(shape, l):
    return pl.BlockSpec((None,) + shape, lambda b, i: (l,) + (0,) * len(shape))


def _vec(width, l=0):
    return _layer((1, width), l)


def _rows(a):
    return a[:, None, :]


def _proj_specs(B, S, l):
    D = D_MODEL
    n_k = S // TK
    in_specs = [
        _vec(D, l),
        _layer((D, D_IN), l),
        _vec(D_SGU, l), _vec(D_SGU, l),
        _layer((CONV_WIDTH, D_CONV), l),
        _vec(D_CONV, l), _vec(D_CONV, l), _vec(D_CONV, l),
    ]
    out_specs = [
        _row(D_CONV),
        pl.BlockSpec((1, N_HEADS, V_DIM, TM), lambda b, i: (b, 0, 0, i)),
        _row(D_ATTN),
        pl.BlockSpec((1, N_HEADS, TM // TK, V_AUG, TK), lambda b, i: (b, 0, i, 0, 0)),
        _row(D_SGU),
        _row(D_SGU),
        _row(D_MIX - D_CONV),
    ]
    out_shape = [
        jax.ShapeDtypeStruct((B, S, D_CONV), BF16),
        jax.ShapeDtypeStruct((B, N_HEADS, V_DIM, S), BF16),
        jax.ShapeDtypeStruct((B, S, D_ATTN), BF16),
        jax.ShapeDtypeStruct((B, N_HEADS, n_k, V_AUG, TK), BF16),
        jax.ShapeDtypeStruct((B, S, D_SGU), F32),
        jax.ShapeDtypeStruct((B, S, D_SGU), BF16),
        jax.ShapeDtypeStruct((B, S, D_MIX - D_CONV), BF16),
    ]
    scratch = [
        pltpu.VMEM((TM + HALO, D_CONV), F32),
        pltpu.VMEM((TM, D_CONV), F32),
    ]
    return in_specs, out_specs, out_shape, scratch


def _in_proj(x, proj_params, l):
    B, S, D = x.shape
    in_specs, out_specs, out_shape, scratch = _proj_specs(B, S, l)
    return pl.pallas_call(
        _in_proj_kernel,
        grid=(B, S // TM),
        in_specs=[_row(D)] + in_specs,
        out_specs=out_specs,
        out_shape=out_shape,
        scratch_shapes=scratch,
        compiler_params=pltpu.CompilerParams(
            dimension_semantics=("arbitrary", "arbitrary"),
            vmem_limit_bytes=VMEM_LIMIT),
        name="in_proj",
    )(x, *proj_params)


def _attn_kernel(lam_init, q_ref, k_ref, vt_ref, pos_ref, slope_ref, lq1_ref, lk1_ref, lq2_ref,
                 lk2_ref, g_ref, o_ref, qt_ref, m_ref, acc_ref, st_ref):
    i = pl.program_id(2)
    coef = slope_ref[0][:, :1] * LOG2E

    zeros = jnp.zeros((QK_DIM, TQ), BF16)
    qt_ref[:QK_DIM, :TQ] = q_ref[0, 0, :QK_DIM, :]
    qt_ref[QK_DIM:V_DIM, :TQ] = zeros
    qt_ref[:QK_DIM, TQ:] = zeros
    qt_ref[QK_DIM:V_DIM, TQ:] = q_ref[0, 0, QK_DIM:, :]

    c0 = coef.astype(BF16).astype(F32)
    c1 = (coef - c0).astype(BF16).astype(F32)
    c2 = (coef - c0 - c1).astype(BF16).astype(F32)
    row = lax.broadcasted_iota(jnp.int32, (V_DIM, 2 * TQ), 0)
    term = row % N_COEF
    terms = jnp.where(term == 0, c0, jnp.where(term == 1, c1, c2))
    qt_ref[V_DIM:, :] = jnp.where(row < 2 * N_COEF, terms, 0.0).astype(BF16)

    m_ref[...] = jnp.full(m_ref.shape, NEG, F32)
    acc_ref[...] = jnp.zeros(acc_ref.shape, F32)

    cols = lambda s: slice(s * STRIP, (s + 1) * STRIP)

    def key_tile(j):
        kb = k_ref[0, pl.ds(pl.multiple_of(j * TK, TK), TK), :]
        return jnp.concatenate([kb, pos_ref[...]], axis=1)

    def score(kaug, columns):
        return jnp.dot(kaug, qt_ref[:, columns], preferred_element_type=F32)

    def run(tiles):
        units = []
        for j, modes in tiles:
            kaug = key_tile(j)
            c = coef * (j * TK - i * TQ).astype(F32)
            units += [(j, kaug, c, s, modes[s]) for s in range(N_STRIPS) if modes[s]]

        def qk(u):
            _, kaug, _, s, mode = units[u]
            st = st_ref.at[u % S_SLOTS]
            if mode == "diag":
                right = slice(s * STRIP + HALF, (s + 1) * STRIP)
                st[:HALF, :] = score(kaug[:HALF], cols(s))
                st[HALF:, HALF:] = score(kaug[HALF:], right)
            else:
                st[...] = score(kaug, cols(s))

        def softmax_pv(u):
            j, _, c, s, mode = units[u]
            st = st_ref.at[u % S_SLOTS]
            m_old = m_ref[:, cols(s)]
            if mode == "diag":
                kr = lax.broadcasted_iota(jnp.int32, (HALF, HALF), 0)
                qr = lax.broadcasted_iota(jnp.int32, (HALF, HALF), 1)
                top = jnp.concatenate(
                    [jnp.where(kr <= qr, st[:HALF, :HALF], NEG), st[:HALF, HALF:]], axis=1)
                bot = jnp.where(kr <= qr, st[HALF:, HALF:], NEG)
                tmax = jnp.max(top, axis=0, keepdims=True)
                bmax = jnp.max(bot, axis=0, keepdims=True)
                blk_max = jnp.concatenate(
                    [tmax[:, :HALF], jnp.maximum(tmax[:, HALF:], bmax)], axis=1)
            else:
                blk_max = jnp.max(st[...], axis=0, keepdims=True)
            m_new = jnp.maximum(m_old, blk_max + c)
            alpha = jnp.exp2(m_old - m_new)
            shift = m_new - c
            if mode == "diag":
                o = jnp.dot(vt_ref[0, 0, j, :, :HALF], jnp.exp2(top - shift).astype(BF16),
                            preferred_element_type=F32)
                o_right = jnp.dot(vt_ref[0, 0, j, :, HALF:],
                                  jnp.exp2(bot - shift[:, HALF:]).astype(BF16),
                                  preferred_element_type=F32)
                o = jnp.concatenate([o[:, :HALF], o[:, HALF:] + o_right], axis=1)
            else:
                o = None
                for h in range(TK // HALF):
                    keys = slice(h * HALF, (h + 1) * HALF)
                    p = jnp.exp2(st[keys, :] - shift).astype(BF16)
                    part = jnp.dot(vt_ref[0, 0, j, :, keys], p, preferred_element_type=F32)
                    o = part if o is None else o + part
            acc_ref[:, cols(s)] = alpha * acc_ref[:, cols(s)] + o
            m_ref[:, cols(s)] = m_new

        for t in range(-QK_AHEAD, len(units)):
            if t + QK_AHEAD < len(units):
                qk(t + QK_AHEAD)
            if t >= 0:
                softmax_pv(t)

    def body(jj, carry):
        run([(jj * Q_STRIPS + d, ["full"] * N_STRIPS) for d in range(Q_STRIPS)])
        return carry

    lax.fori_loop(0, i, body, 0)
    run([(i * Q_STRIPS + d,
          [None if s % Q_STRIPS < d else "diag" if s % Q_STRIPS == d else "full"
           for s in range(N_STRIPS)]) for d in range(Q_STRIPS)])

    lam = (jnp.exp(jnp.sum(lq1_ref[...] * lk1_ref[...], axis=-1, keepdims=True))
           - jnp.exp(jnp.sum(lq2_ref[...] * lk2_ref[...], axis=-1, keepdims=True))
           + lam_init)
    inv = 1.0 / acc_ref[V_DIM:V_DIM + 1, :]
    od = (acc_ref[:V_DIM, :TQ] * inv[:, :TQ]
          - acc_ref[:V_DIM, TQ:] * (lam * inv[:, TQ:]))
    ms = jnp.mean(od * od, axis=0, keepdims=True)
    y = (od * lax.rsqrt(ms + EPS)).T
    o_ref[0] = ((y * g_ref[...]) * (1.0 - lam_init)).astype(BF16)


def _key_positions():
    jr = jnp.arange(TK, dtype=jnp.int32)
    lo = (jr % POS_SPLIT).astype(F32)
    hi = (jr - jr % POS_SPLIT).astype(F32)
    pos = jnp.stack([lo] * N_COEF + [hi] * N_COEF, axis=1)
    return jnp.pad(pos, ((0, 0), (0, V_DIM - 2 * N_COEF))).astype(BF16)


def _attention(q, k, vt, pos, slopes, lq1, lk1, lq2, lk2, subln_g, l, lam_init):
    B, S, _ = k.shape
    n_k = S // TK
    grid = (B, N_HEADS, S // TQ)
    small = lambda width: pl.BlockSpec((None, 1, width), lambda b, h, i: (l, 0, 0))
    return pl.pallas_call(
        functools.partial(_attn_kernel, lam_init),
        grid=grid,
        in_specs=[
            pl.BlockSpec((1, 1, V_DIM, TQ), lambda b, h, i: (b, h, 0, i)),
            pl.BlockSpec((1, S, V_DIM), lambda b, h, i: (b, 0, h)),
            pl.BlockSpec((1, 1, n_k, V_AUG, TK), lambda b, h, i: (b, h, 0, 0, 0)),
            pl.BlockSpec((TK, V_DIM), lambda b, h, i: (0, 0)),
            pl.BlockSpec((1, 1, 128), lambda b, h, i: (h, 0, 0)),
            small(QK_DIM), small(QK_DIM), small(QK_DIM), small(QK_DIM),
            small(V_DIM),
        ],
        out_specs=pl.BlockSpec((1, TQ, V_DIM), lambda b, h, i: (b, i, h)),
        out_shape=jax.ShapeDtypeStruct((B, S, D_ATTN), BF16),
        scratch_shapes=[
            pltpu.VMEM((2 * V_DIM, 2 * TQ), BF16),
            pltpu.VMEM((1, 2 * TQ), F32),
            pltpu.VMEM((V_AUG, 2 * TQ), F32),
            pltpu.VMEM((S_SLOTS, TK, STRIP), F32),
        ],
        compiler_params=pltpu.CompilerParams(
            dimension_semantics=("arbitrary", "arbitrary", "arbitrary"),
            vmem_limit_bytes=VMEM_LIMIT),
        name="diff_attn",
    )(q, k, vt, pos, slopes, lq1, lk1, lq2, lk2, subln_g)


def _mix_body(x_ref, ya_ref, yb_ref, u_ref, vv_ref, sgate_ref, ws_ref, bs_ref, wout_ref,
              ybuf_ref):
    ybuf_ref[:, :D_CONV] = ya_ref[0]

    ybuf_ref[:, D_CONV:D_CONV + D_ATTN] = yb_ref[0] * sgate_ref[0, :, :D_ATTN]

    tr = lax.broadcasted_iota(jnp.int32, (CHUNK, CHUNK), 0)
    tc = lax.broadcasted_iota(jnp.int32, (CHUNK, CHUNK), 1)
    wm = [jnp.where(tr >= tc, ws_ref[g], 0.0).astype(BF16) for g in range(N_GROUPS)]
    lane_group = lax.broadcasted_iota(jnp.int32, (CHUNK, D_SGU), 1) // GROUP_DIM
    for c in range(TM // CHUNK):
        rows = slice(c * CHUNK, (c + 1) * CHUNK)
        vc = vv_ref[0, rows, :]
        mixed = jnp.zeros((CHUNK, D_SGU), F32)
        for g in range(N_GROUPS):
            r = jnp.dot(wm[g], vc, preferred_element_type=F32)
            mixed = jnp.where(lane_group == g, r, mixed)
        y_c = u_ref[0, rows, :] * (mixed + bs_ref[...])
        ybuf_ref[rows, D_CONV + D_ATTN:] = (
            y_c * sgate_ref[0, rows, D_ATTN:].astype(F32)).astype(BF16)

    return x_ref[0] + jnp.dot(ybuf_ref[...], wout_ref[...], preferred_element_type=F32)


N_MIX_IN = 9
N_PROJ_IN = 8


def _mix_final_kernel(*refs):
    mix_in, (nf_ref, o_ref, ybuf_ref) = refs[:N_MIX_IN], refs[N_MIX_IN:]
    out = _mix_body(*mix_in, ybuf_ref)
    ms = jnp.mean(out * out, axis=-1, keepdims=True)
    o_ref[0] = (out * lax.rsqrt(ms + EPS)) * nf_ref[...]


def _mix_proj_kernel(*refs):
    mix_in = refs[:N_MIX_IN]
    proj_in = refs[N_MIX_IN:N_MIX_IN + N_PROJ_IN]
    xo_ref, *proj_out = refs[N_MIX_IN + N_PROJ_IN:-3]
    ybuf_ref, zbuf_ref, sgate_a_ref = refs[-3:]
    out = _mix_body(*mix_in, ybuf_ref)
    xo_ref[0] = out
    _proj_body(out, *proj_in, *proj_out, zbuf_ref, sgate_a_ref)


def _mix_specs(l):
    D = D_MODEL
    in_specs = [
        _row(D),
        _row(D_CONV),
        _row(D_ATTN),
        _row(D_SGU),
        _row(D_SGU),
        _row(D_MIX - D_CONV),
        _layer((N_GROUPS, CHUNK, CHUNK), l),
        _layer((CHUNK, D_SGU), l),
        _layer((D_MIX, D), l),
    ]
    scratch = [pltpu.VMEM((TM, D_MIX), BF16)]
    return in_specs, scratch


def _mix_final(mix_args, nf, l):
    B, S, D = mix_args[0].shape
    in_specs, scratch = _mix_specs(l)
    return pl.pallas_call(
        _mix_final_kernel,
        grid=(B, S // TM),
        in_specs=in_specs + [_vec(D)],
        out_specs=_row(D),
        out_shape=jax.ShapeDtypeStruct((B, S, D), F32),
        scratch_shapes=scratch,
        compiler_params=pltpu.CompilerParams(
            dimension_semantics=("arbitrary", "arbitrary"),
            vmem_limit_bytes=VMEM_LIMIT),
        name="mix_final",
    )(*mix_args, nf)


def _mix_proj(mix_args, proj_params, l):
    B, S, D = mix_args[0].shape
    mix_in, mix_scratch = _mix_specs(l)
    proj_in, proj_out, proj_shape, proj_scratch = _proj_specs(B, S, l + 1)
    outs = pl.pallas_call(
        _mix_proj_kernel,
        grid=(B, S // TM),
        in_specs=mix_in + proj_in,
        out_specs=[_row(D)] + proj_out,
        out_shape=[jax.ShapeDtypeStruct((B, S, D), F32)] + proj_shape,
        scratch_shapes=mix_scratch + proj_scratch,
        compiler_params=pltpu.CompilerParams(
            dimension_semantics=("arbitrary", "arbitrary"),
            vmem_limit_bytes=VMEM_LIMIT),
        name="mix_proj",
    )(*mix_args, *proj_params)
    return outs[0], outs[1:]


def kernel(x, norm_g, w_in, conv_w, conv_b, conv_ln_g, conv_ln_b, lam_q1, lam_k1, lam_q2, lam_k2,
           subln_g, sgu_ln_g, sgu_ln_b, w_s, b_s, w_out, norm_f):
    depth = w_in.shape[0]
    slopes = jnp.exp2(-8.0 * (jnp.arange(N_HEADS, dtype=F32) + 1.0) / N_HEADS)
    slopes = jnp.broadcast_to(slopes[:, None, None], (N_HEADS, 1, 128))
    pos = _key_positions()
    proj_params = (_rows(norm_g), w_in.astype(BF16), _rows(sgu_ln_g), _rows(sgu_ln_b),
                   conv_w, _rows(conv_b), _rows(conv_ln_g), _rows(conv_ln_b))
    attn_params = tuple(_rows(a) for a in (lam_q1, lam_k1, lam_q2, lam_k2, subln_g))
    bs_exp = jnp.repeat(jnp.swapaxes(b_s, 1, 2), GROUP_DIM, axis=2)
    mix_params = (w_s, bs_exp, w_out.astype(BF16))

    ya, q, k, vt, u, vv, sgate = _in_proj(x, proj_params, 0)
    for l in range(depth):
        lam_init = 0.8 - 0.6 * math.exp(-0.3 * l)
        yb = _attention(q, k, vt, pos, slopes, *attn_params, l, lam_init)
        mix_args = (x, ya, yb, u, vv, sgate) + mix_params
        if l == depth - 1:
            return _mix_final(mix_args, norm_f.reshape(1, 1, -1), l)
        x, (ya, q, k, vt, u, vv, sgate) = _mix_proj(mix_args, proj_params, l)
```

```python
import functools
import math

import jax
import jax.numpy as jnp
import numpy as np
from jax import lax
from jax.experimental import pallas as pl
from jax.experimental.pallas import tpu as pltpu

D_MODEL = 1024
D_MIX = D_MODEL
D_CONV = D_MIX // 4
D_ATTN = D_MIX // 2
D_SGU = D_MIX // 4
N_HEADS = 4
V_DIM = D_ATTN // N_HEADS
QK_DIM = V_DIM // 2
CONV_WIDTH = 31
N_GROUPS = 4
GROUP_DIM = D_SGU // N_GROUPS
CHUNK = 128
D_IN = 2 * D_CONV + 3 * D_ATTN + 2 * D_SGU + D_MIX
EPS = 1e-6
NEG = -1e30

C_CONV = 0
C_Q = C_CONV + 2 * D_CONV
C_K = C_Q + D_ATTN
C_V = C_K + D_ATTN
C_SGU = C_V + D_ATTN
C_GATE = C_SGU + 2 * D_SGU

TM = 512
TQ = 2048
TK = 512
STRIP = TK
Q_STRIPS = TQ // STRIP
N_STRIPS = 2 * Q_STRIPS
QK_AHEAD = 4
S_SLOTS = QK_AHEAD + 1
HALF = TK // 2
SUBLANES = 8
HALO = 32
CONV_ROWS = TM // 4
V_AUG = V_DIM + 16
N_COEF = 3
POS_SPLIT = 256
LOG2E = 1.0 / math.log(2.0)
VMEM_LIMIT = 56 * 1024 * 1024

F32 = jnp.float32
BF16 = jnp.bfloat16


def _layer_norm(v, g, b):
    mu = jnp.mean(v, axis=-1, keepdims=True)
    vc = v - mu
    var = jnp.mean(vc * vc, axis=-1, keepdims=True)
    return (vc * lax.rsqrt(var + EPS)) * g + b


def _silu(v):
    return v * jax.nn.sigmoid(v)


def _causal_conv(zbuf_ref, cw_ref, bias, r0, n):
    base = HALO - (CONV_WIDTH - 1)
    conv = jnp.zeros((n, D_CONV), F32) + bias
    for b in range(SUBLANES):
        rows = n if b == 0 else n + SUBLANES
        part = None
        for w in range(CONV_WIDTH):
            if (base + w) % SUBLANES != b:
                continue
            term = zbuf_ref[pl.ds(r0 + base + w - b, rows), :] * cw_ref[w:w + 1, :]
            part = term if part is None else part + term
        conv = conv + (part if b == 0 else part[b:b + n, :])
    return conv


def _zero_after(v):
    bits = pltpu.bitcast(v[:SUBLANES, -D_CONV:], jnp.uint32)
    return ((bits >> 16) >> 16)[:1, :].astype(F32)


def _proj_body(x, g_ref, w_ref, slg_ref, slb_ref, cw_ref, cb_ref, clg_ref, clb_ref,
               ya_ref, q_ref, k_ref, vt_ref, u_ref, vv_ref, sgate_ref, zbuf_ref, sgate_a_ref):
    ms = jnp.mean(x * x, axis=-1, keepdims=True)
    h = ((x * lax.rsqrt(ms + EPS)) * g_ref[...]).astype(BF16)

    def proj(lo, width):
        return jnp.dot(h, w_ref[:, lo:lo + width].astype(BF16), preferred_element_type=F32)

    @pl.when(pl.program_id(1) == 0)
    def _():
        zbuf_ref[:HALO, :] = jnp.zeros((HALO, D_CONV), F32)

    def conv_chunk(c, after):
        r0, n = c * CONV_ROWS, CONV_ROWS
        conv = _causal_conv(zbuf_ref, cw_ref, cb_ref[...] + _zero_after(after), r0, n)
        y_a = _silu(_layer_norm(conv, clg_ref[...], clb_ref[...]))
        ya_ref[0, r0:r0 + n, :] = (y_a * sgate_a_ref[r0:r0 + n, :]).astype(BF16)

    pc = proj(C_CONV, 2 * D_CONV)
    zbuf_ref[HALO:, :] = pc[:, :D_CONV] * jax.nn.sigmoid(pc[:, D_CONV:])
    ga = proj(C_GATE, D_CONV)
    sgate_a_ref[...] = _silu(ga)
    conv_chunk(0, ga)
    qp = proj(C_Q, D_ATTN)
    for hd in range(N_HEADS):
        qh = qp[:, hd * V_DIM:(hd + 1) * V_DIM] * (LOG2E / math.sqrt(QK_DIM))
        q_ref[0, hd] = qh.T.astype(BF16)
    conv_chunk(1, qp)
    kp = proj(C_K, D_ATTN)
    k_ref[0] = kp.astype(BF16)
    conv_chunk(2, kp)
    v = proj(C_V, D_ATTN)
    for hd in range(N_HEADS):
        for c in range(TM // TK):
            blk = v[c * TK:(c + 1) * TK, hd * V_DIM:(hd + 1) * V_DIM]
            vt_ref[0, hd, c, :V_DIM, :] = blk.T.astype(BF16)
            vt_ref[0, hd, c, V_DIM:, :] = jnp.ones((V_AUG - V_DIM, TK), BF16)
    conv_chunk(3, v)
    ps = proj(C_SGU, 2 * D_SGU)
    u_ref[0] = ps[:, :D_SGU]
    vv_ref[0] = _layer_norm(ps[:, D_SGU:], slg_ref[...], slb_ref[...]).astype(BF16)
    sgate_ref[0] = _silu(proj(C_GATE + D_CONV, D_MIX - D_CONV)).astype(BF16)
    zbuf_ref[:HALO, :] = zbuf_ref[TM:, :]


def _in_proj_kernel(x_ref, *refs):
    _proj_body(x_ref[0], *refs)


def _row(width):
    return pl.BlockSpec((1, TM, width), lambda b, i: (b, i, 0))


def _layer(shape, l):
    return pl.BlockSpec((None,) + shape, lambda b, i: (l,) + (0,) * len(shape))


def _vec(width, l=0):
    return _layer((1, width), l)


def _rows(a):
    return a[:, None, :]


def _proj_specs(B, S, l):
    D = D_MODEL
    n_k = S // TK
    in_specs = [
        _vec(D, l),
        _layer((D, D_IN), l),
        _vec(D_SGU, l), _vec(D_SGU, l),
        _layer((CONV_WIDTH, D_CONV), l),
        _vec(D_CONV, l), _vec(D_CONV, l), _vec(D_CONV, l),
    ]
    out_specs = [
        _row(D_CONV),
        pl.BlockSpec((1, N_HEADS, V_DIM, TM), lambda b, i: (b, 0, 0, i)),
        _row(D_ATTN),
        pl.BlockSpec((1, N_HEADS, TM // TK, V_AUG, TK), lambda b, i: (b, 0, i, 0, 0)),
        _row(D_SGU),
        _row(D_SGU),
        _row(D_MIX - D_CONV),
    ]
    out_shape = [
        jax.ShapeDtypeStruct((B, S, D_CONV), BF16),
        jax.ShapeDtypeStruct((B, N_HEADS, V_DIM, S), BF16),
        jax.ShapeDtypeStruct((B, S, D_ATTN), BF16),
        jax.ShapeDtypeStruct((B, N_HEADS, n_k, V_AUG, TK), BF16),
        jax.ShapeDtypeStruct((B, S, D_SGU), F32),
        jax.ShapeDtypeStruct((B, S, D_SGU), BF16),
        jax.ShapeDtypeStruct((B, S, D_MIX - D_CONV), BF16),
    ]
    scratch = [
        pltpu.VMEM((TM + HALO, D_CONV), F32),
        pltpu.VMEM((TM, D_CONV), F32),
    ]
    return in_specs, out_specs, out_shape, scratch


def _in_proj(x, proj_params, l):
    B, S, D = x.shape
    in_specs, out_specs, out_shape, scratch = _proj_specs(B, S, l)
    return pl.pallas_call(
        _in_proj_kernel,
        grid=(B, S // TM),
        in_specs=[_row(D)] + in_specs,
        out_specs=out_specs,
        out_shape=out_shape,
        scratch_shapes=scratch,
        compiler_params=pltpu.CompilerParams(
            dimension_semantics=("arbitrary", "arbitrary"),
            vmem_limit_bytes=VMEM_LIMIT),
        name="in_proj",
    )(x, *proj_params)


def _attn_kernel(lam_init, q_ref, k_ref, vt_ref, pos_ref, slope_ref, lq1_ref, lk1_ref, lq2_ref,
                 lk2_ref, g_ref, o_ref, qt_ref, m_ref, acc_ref, st_ref):
    i = pl.program_id(2)
    coef = slope_ref[0][:, :1] * LOG2E

    zeros = jnp.zeros((QK_DIM, TQ), BF16)
    qt_ref[:QK_DIM, :TQ] = q_ref[0, 0, :QK_DIM, :]
    qt_ref[QK_DIM:V_DIM, :TQ] = zeros
    qt_ref[:QK_DIM, TQ:] = zeros
    qt_ref[QK_DIM:V_DIM, TQ:] = q_ref[0, 0, QK_DIM:, :]

    c0 = coef.astype(BF16).astype(F32)
    c1 = (coef - c0).astype(BF16).astype(F32)
    c2 = (coef - c0 - c1).astype(BF16).astype(F32)
    row = lax.broadcasted_iota(jnp.int32, (V_DIM, 2 * TQ), 0)
    term = row % N_COEF
    terms = jnp.where(term == 0, c0, jnp.where(term == 1, c1, c2))
    qt_ref[V_DIM:, :] = jnp.where(row < 2 * N_COEF, terms, 0.0).astype(BF16)

    m_ref[...] = jnp.full(m_ref.shape, NEG, F32)
    acc_ref[...] = jnp.zeros(acc_ref.shape, F32)

    cols = lambda s: slice(s * STRIP, (s + 1) * STRIP)

    def key_tile(j):
        kb = k_ref[0, pl.ds(pl.multiple_of(j * TK, TK), TK), :]
        return jnp.concatenate([kb, pos_ref[...]], axis=1)

    def score(kaug, columns):
        return jnp.dot(kaug, qt_ref[:, columns], preferred_element_type=F32)

    def run(tiles):
        units = []
        for j, modes in tiles:
            kaug = key_tile(j)
            c = coef * (j * TK - i * TQ).astype(F32)
            units += [(j, kaug, c, s, modes[s]) for s in range(N_STRIPS) if modes[s]]

        def qk(u):
            _, kaug, _, s, mode = units[u]
            st = st_ref.at[u % S_SLOTS]
            if mode == "diag":
                right = slice(s * STRIP + HALF, (s + 1) * STRIP)
                st[:HALF, :] = score(kaug[:HALF], cols(s))
                st[HALF:, HALF:] = score(kaug[HALF:], right)
            else:
                st[...] = score(kaug, cols(s))

        def softmax_pv(u):
            j, _, c, s, mode = units[u]
            st = st_ref.at[u % S_SLOTS]
            m_old = m_ref[:, cols(s)]
            if mode == "diag":
                kr = lax.broadcasted_iota(jnp.int32, (HALF, HALF), 0)
                qr = lax.broadcasted_iota(jnp.int32, (HALF, HALF), 1)
                top = jnp.concatenate(
                    [jnp.where(kr <= qr, st[:HALF, :HALF], NEG), st[:HALF, HALF:]], axis=1)
                bot = jnp.where(kr <= qr, st[HALF:, HALF:], NEG)
                tmax = jnp.max(top, axis=0, keepdims=True)
                bmax = jnp.max(bot, axis=0, keepdims=True)
                blk_max = jnp.concatenate(
                    [tmax[:, :HALF], jnp.maximum(tmax[:, HALF:], bmax)], axis=1)
            else:
                blk_max = jnp.max(st[...], axis=0, keepdims=True)
            m_new = jnp.maximum(m_old, blk_max + c)
            alpha = jnp.exp2(m_old - m_new)
            shift = m_new - c
            if mode == "diag":
                o = jnp.dot(vt_ref[0, 0, j, :, :HALF], jnp.exp2(top - shift).astype(BF16),
                            preferred_element_type=F32)
                o_right = jnp.dot(vt_ref[0, 0, j, :, HALF:],
                                  jnp.exp2(bot - shift[:, HALF:]).astype(BF16),
                                  preferred_element_type=F32)
                o = jnp.concatenate([o[:, :HALF], o[:, HALF:] + o_right], axis=1)
            else:
                o = None
                for h in range(TK // HALF):
                    keys = slice(h * HALF, (h + 1) * HALF)
                    p = jnp.exp2(st[keys, :] - shift).astype(BF16)
                    part = jnp.dot(vt_ref[0, 0, j, :, keys], p, preferred_element_type=F32)
                    o = part if o is None else o + part
            acc_ref[:, cols(s)] = alpha * acc_ref[:, cols(s)] + o
            m_ref[:, cols(s)] = m_new

        for t in range(-QK_AHEAD, len(units)):
            if t + QK_AHEAD < len(units):
                qk(t + QK_AHEAD)
            if t >= 0:
                softmax_pv(t)

    def body(jj, carry):
        run([(jj * Q_STRIPS + d, ["full"] * N_STRIPS) for d in range(Q_STRIPS)])
        return carry

    lax.fori_loop(0, i, body, 0)
    run([(i * Q_STRIPS + d,
          [None if s % Q_STRIPS < d else "diag" if s % Q_STRIPS == d else "full"
           for s in range(N_STRIPS)]) for d in range(Q_STRIPS)])

    lam = (jnp.exp(jnp.sum(lq1_ref[...] * lk1_ref[...], axis=-1, keepdims=True))
           - jnp.exp(jnp.sum(lq2_ref[...] * lk2_ref[...], axis=-1, keepdims=True))
           + lam_init)
    inv = 1.0 / acc_ref[V_DIM:V_DIM + 1, :]
    od = (acc_ref[:V_DIM, :TQ] * inv[:, :TQ]
          - acc_ref[:V_DIM, TQ:] * (lam * inv[:, TQ:]))
    ms = jnp.mean(od * od, axis=0, keepdims=True)
    y = (od * lax.rsqrt(ms + EPS)).T
    o_ref[0] = ((y * g_ref[...]) * (1.0 - lam_init)).astype(BF16)


def _key_positions():
    jr = np.arange(TK)
    pos = np.zeros((TK, V_DIM), np.float32)
    pos[:, :N_COEF] = (jr % POS_SPLIT)[:, None]
    pos[:, N_COEF:2 * N_COEF] = (jr - jr % POS_SPLIT)[:, None]
    return jnp.asarray(pos, dtype=BF16)


def _alibi_slopes():
    slopes = np.exp2(-8.0 * (np.arange(N_HEADS, dtype=np.float32) + 1.0) / N_HEADS)
    return jnp.asarray(np.broadcast_to(slopes[:, None, None], (N_HEADS, 1, 128)), dtype=F32)


def _attention(q, k, vt, pos, slopes, lq1, lk1, lq2, lk2, subln_g, l, lam_init):
    B, S, _ = k.shape
    n_k = S // TK
    grid = (B, N_HEADS, S // TQ)
    small = lambda width: pl.BlockSpec((None, 1, width), lambda b, h, i: (l, 0, 0))
    return pl.pallas_call(
        functools.partial(_attn_kernel, lam_init),
        grid=grid,
        in_specs=[
            pl.BlockSpec((1, 1, V_DIM, TQ), lambda b, h, i: (b, h, 0, i)),
            pl.BlockSpec((1, S, V_DIM), lambda b, h, i: (b, 0, h)),
            pl.BlockSpec((1, 1, n_k, V_AUG, TK), lambda b, h, i: (b, h, 0, 0, 0)),
            pl.BlockSpec((TK, V_DIM), lambda b, h, i: (0, 0)),
            pl.BlockSpec((1, 1, 128), lambda b, h, i: (h, 0, 0)),
            small(QK_DIM), small(QK_DIM), small(QK_DIM), small(QK_DIM),
            small(V_DIM),
        ],
        out_specs=pl.BlockSpec((1, TQ, V_DIM), lambda b, h, i: (b, i, h)),
        out_shape=jax.ShapeDtypeStruct((B, S, D_ATTN), BF16),
        scratch_shapes=[
            pltpu.VMEM((2 * V_DIM, 2 * TQ), BF16),
            pltpu.VMEM((1, 2 * TQ), F32),
            pltpu.VMEM((V_AUG, 2 * TQ), F32),
            pltpu.VMEM((S_SLOTS, TK, STRIP), F32),
        ],
        compiler_params=pltpu.CompilerParams(
            dimension_semantics=("arbitrary", "arbitrary", "arbitrary"),
            vmem_limit_bytes=VMEM_LIMIT),
        name="diff_attn",
    )(q, k, vt, pos, slopes, lq1, lk1, lq2, lk2, subln_g)


def _mix_body(x_ref, ya_ref, yb_ref, u_ref, vv_ref, sgate_ref, ws_ref, bs_ref, wout_ref,
              ybuf_ref):
    ybuf_ref[:, :D_CONV] = ya_ref[0]

    ybuf_ref[:, D_CONV:D_CONV + D_ATTN] = yb_ref[0] * sgate_ref[0, :, :D_ATTN]

    tr = lax.broadcasted_iota(jnp.int32, (CHUNK, CHUNK), 0)
    tc = lax.broadcasted_iota(jnp.int32, (CHUNK, CHUNK), 1)
    wm = [jnp.where(tr >= tc, ws_ref[g], 0.0).astype(BF16) for g in range(N_GROUPS)]
    lane_group = lax.broadcasted_iota(jnp.int32, (CHUNK, D_SGU), 1) // GROUP_DIM
    for c in range(TM // CHUNK):
        rows = slice(c * CHUNK, (c + 1) * CHUNK)
        vc = vv_ref[0, rows, :]
        mixed = jnp.zeros((CHUNK, D_SGU), F32)
        for g in range(N_GROUPS):
            r = jnp.dot(wm[g], vc, preferred_element_type=F32)
            mixed = jnp.where(lane_group == g, r, mixed)
        y_c = u_ref[0, rows, :] * (mixed + bs_ref[...])
        ybuf_ref[rows, D_CONV + D_ATTN:] = (
            y_c * sgate_ref[0, rows, D_ATTN:].astype(F32)).astype(BF16)

    return x_ref[0] + jnp.dot(ybuf_ref[...], wout_ref[...].astype(BF16),
                              preferred_element_type=F32)


N_MIX_IN = 9
N_PROJ_IN = 8


def _mix_final_kernel(*refs):
    mix_in, (nf_ref, o_ref, ybuf_ref) = refs[:N_MIX_IN], refs[N_MIX_IN:]
    out = _mix_body(*mix_in, ybuf_ref)
    ms = jnp.mean(out * out, axis=-1, keepdims=True)
    o_ref[0] = (out * lax.rsqrt(ms + EPS)) * nf_ref[...]


def _mix_proj_kernel(*refs):
    mix_in = refs[:N_MIX_IN]
    proj_in = refs[N_MIX_IN:N_MIX_IN + N_PROJ_IN]
    xo_ref, *proj_out = refs[N_MIX_IN + N_PROJ_IN:-3]
    ybuf_ref, zbuf_ref, sgate_a_ref = refs[-3:]
    out = _mix_body(*mix_in, ybuf_ref)
    xo_ref[0] = out
    _proj_body(out, *proj_in, *proj_out, zbuf_ref, sgate_a_ref)


def _mix_specs(l):
    D = D_MODEL
    in_specs = [
        _row(D),
        _row(D_CONV),
        _row(D_ATTN),
        _row(D_SGU),
        _row(D_SGU),
        _row(D_MIX - D_CONV),
        _layer((N_GROUPS, CHUNK, CHUNK), l),
        _layer((CHUNK, D_SGU), l),
        _layer((D_MIX, D), l),
    ]
    scratch = [pltpu.VMEM((TM, D_MIX), BF16)]
    return in_specs, scratch


def _mix_final(mix_args, nf, l):
    B, S, D = mix_args[0].shape
    in_specs, scratch = _mix_specs(l)
    return pl.pallas_call(
        _mix_final_kernel,
        grid=(B, S // TM),
        in_specs=in_specs + [_vec(D)],
        out_specs=_row(D),
        out_shape=jax.ShapeDtypeStruct((B, S, D), F32),
        scratch_shapes=scratch,
        compiler_params=pltpu.CompilerParams(
            dimension_semantics=("arbitrary", "arbitrary"),
            vmem_limit_bytes=VMEM_LIMIT),
        name="mix_final",
    )(*mix_args, nf)


def _mix_proj(mix_args, proj_params, l):
    B, S, D = mix_args[0].shape
    mix_in, mix_scratch = _mix_specs(l)
    proj_in, proj_out, proj_shape, proj_scratch = _proj_specs(B, S, l + 1)
    outs = pl.pallas_call(
        _mix_proj_kernel,
        grid=(B, S // TM),
        in_specs=mix_in + proj_in,
        out_specs=[_row(D)] + proj_out,
        out_shape=[jax.ShapeDtypeStruct((B, S, D), F32)] + proj_shape,
        scratch_shapes=mix_scratch + proj_scratch,
        compiler_params=pltpu.CompilerParams(
            dimension_semantics=("arbitrary", "arbitrary"),
            vmem_limit_bytes=VMEM_LIMIT),
        name="mix_proj",
    )(*mix_args, *proj_params)
    return outs[0], outs[1:]


def kernel(x, norm_g, w_in, conv_w, conv_b, conv_ln_g, conv_ln_b, lam_q1, lam_k1, lam_q2, lam_k2,
           subln_g, sgu_ln_g, sgu_ln_b, w_s, b_s, w_out, norm_f):
    depth = w_in.shape[0]
    slopes = _alibi_slopes()
    pos = _key_positions()
    proj_params = (_rows(norm_g), w_in, _rows(sgu_ln_g), _rows(sgu_ln_b),
                   conv_w, _rows(conv_b), _rows(conv_ln_g), _rows(conv_ln_b))
    attn_params = tuple(_rows(a) for a in (lam_q1, lam_k1, lam_q2, lam_k2, subln_g))
    bs_exp = jnp.repeat(jnp.swapaxes(b_s, 1, 2), GROUP_DIM, axis=2)
    mix_params = (w_s, bs_exp, w_out)

    ya, q, k, vt, u, vv, sgate = _in_proj(x, proj_params, 0)
    for l in range(depth):
        lam_init = 0.8 - 0.6 * math.exp(-0.3 * l)
        yb = _attention(q, k, vt, pos, slopes, *attn_params, l, lam_init)
        mix_args = (x, ya, yb, u, vv, sgate) + mix_params
        if l == depth - 1:
            return _mix_final(mix_args, norm_f.reshape(1, 1, -1), l)
        x, (ya, q, k, vt, u, vv, sgate) = _mix_proj(mix_args, proj_params, l)
```

```python
import functools
import math

import jax
import jax.numpy as jnp
import numpy as np
from jax import lax
from jax.experimental import pallas as pl
from jax.experimental.pallas import tpu as pltpu

D_MODEL = 1024
D_MIX = D_MODEL
D_CONV = D_MIX // 4
D_ATTN = D_MIX // 2
D_SGU = D_MIX // 4
N_HEADS = 4
V_DIM = D_ATTN // N_HEADS
QK_DIM = V_DIM // 2
CONV_WIDTH = 31
N_GROUPS = 4
GROUP_DIM = D_SGU // N_GROUPS
CHUNK = 128
D_IN = 2 * D_CONV + 3 * D_ATTN + 2 * D_SGU + D_MIX
EPS = 1e-6
NEG = -1e30

C_CONV = 0
C_Q = C_CONV + 2 * D_CONV
C_K = C_Q + D_ATTN
C_V = C_K + D_ATTN
C_SGU = C_V + D_ATTN
C_GATE = C_SGU + 2 * D_SGU

TM = 512
TQ = 2048
TK = 512
STRIP = TK
Q_STRIPS = TQ // STRIP
N_STRIPS = 2 * Q_STRIPS
QK_AHEAD = 4
S_SLOTS = QK_AHEAD + 1
HALF = TK // 2
SUBLANES = 8
HALO = 32
CONV_ROWS = TM // 4
V_AUG = V_DIM + 16
N_COEF = 3
POS_SPLIT = 256
LOG2E = 1.0 / math.log(2.0)
VMEM_LIMIT = 56 * 1024 * 1024

F32 = jnp.float32
BF16 = jnp.bfloat16


def _layer_norm(v, g, b):
    mu = jnp.mean(v, axis=-1, keepdims=True)
    vc = v - mu
    var = jnp.mean(vc * vc, axis=-1, keepdims=True)
    return (vc * lax.rsqrt(var + EPS)) * g + b


def _silu(v):
    return v * jax.nn.sigmoid(v)


def _causal_conv(zbuf_ref, cw_ref, bias, r0, n):
    base = HALO - (CONV_WIDTH - 1)
    conv = jnp.zeros((n, D_CONV), F32) + bias
    for b in range(SUBLANES):
        rows = n if b == 0 else n + SUBLANES
        part = None
        for w in range(CONV_WIDTH):
            if (base + w) % SUBLANES != b:
                continue
            term = zbuf_ref[pl.ds(r0 + base + w - b, rows), :] * cw_ref[w:w + 1, :]
            part = term if part is None else part + term
        conv = conv + (part if b == 0 else part[b:b + n, :])
    return conv


def _zero_after(v):
    bits = pltpu.bitcast(v[:SUBLANES, -D_CONV:], jnp.uint32)
    return ((bits >> 16) >> 16)[:1, :].astype(F32)


def _proj_body(l, x, g_ref, w_ref, slg_ref, slb_ref, cw_ref, cb_ref, clg_ref, clb_ref,
               ya_ref, q_ref, k_ref, vt_ref, u_ref, vv_ref, sgate_ref, zbuf_ref, sgate_a_ref):
    g_ref, slg_ref, slb_ref, cb_ref, clg_ref, clb_ref = (
        _row_of(r, l) for r in (g_ref, slg_ref, slb_ref, cb_ref, clg_ref, clb_ref))
    ms = jnp.mean(x * x, axis=-1, keepdims=True)
    h = ((x * lax.rsqrt(ms + EPS)) * g_ref[...]).astype(BF16)

    def proj(lo, width):
        return jnp.dot(h, w_ref[:, lo:lo + width].astype(BF16), preferred_element_type=F32)

    @pl.when(pl.program_id(1) == 0)
    def _():
        zbuf_ref[:HALO, :] = jnp.zeros((HALO, D_CONV), F32)

    def conv_chunk(c, after):
        r0, n = c * CONV_ROWS, CONV_ROWS
        conv = _causal_conv(zbuf_ref, cw_ref, cb_ref[...] + _zero_after(after), r0, n)
        y_a = _silu(_layer_norm(conv, clg_ref[...], clb_ref[...]))
        ya_ref[0, r0:r0 + n, :] = (y_a * sgate_a_ref[r0:r0 + n, :]).astype(BF16)

    pc = proj(C_CONV, 2 * D_CONV)
    zbuf_ref[HALO:, :] = pc[:, :D_CONV] * jax.nn.sigmoid(pc[:, D_CONV:])
    ga = proj(C_GATE, D_CONV)
    sgate_a_ref[...] = _silu(ga)
    conv_chunk(0, ga)
    qp = proj(C_Q, D_ATTN)
    for hd in range(N_HEADS):
        qh = qp[:, hd * V_DIM:(hd + 1) * V_DIM] * (LOG2E / math.sqrt(QK_DIM))
        q_ref[0, hd] = qh.T.astype(BF16)
    conv_chunk(1, qp)
    kp = proj(C_K, D_ATTN)
    k_ref[0] = kp.astype(BF16)
    conv_chunk(2, kp)
    v = proj(C_V, D_ATTN)
    for hd in range(N_HEADS):
        for c in range(TM // TK):
            blk = v[c * TK:(c + 1) * TK, hd * V_DIM:(hd + 1) * V_DIM]
            vt_ref[0, hd, c, :V_DIM, :] = blk.T.astype(BF16)
            vt_ref[0, hd, c, V_DIM:, :] = jnp.ones((V_AUG - V_DIM, TK), BF16)
    conv_chunk(3, v)
    ps = proj(C_SGU, 2 * D_SGU)
    u_ref[0] = ps[:, :D_SGU]
    vv_ref[0] = _layer_norm(ps[:, D_SGU:], slg_ref[...], slb_ref[...]).astype(BF16)
    sgate_ref[0] = _silu(proj(C_GATE + D_CONV, D_MIX - D_CONV)).astype(BF16)
    zbuf_ref[:HALO, :] = zbuf_ref[TM:, :]


def _in_proj_kernel(l, x_ref, *refs):
    _proj_body(l, x_ref[0], *refs)


def _row(width):
    return pl.BlockSpec((1, TM, width), lambda b, i: (b, i, 0))


def _layer(shape, l):
    return pl.BlockSpec((None,) + shape, lambda b, i: (l,) + (0,) * len(shape))


def _vec(width, layers):
    return pl.BlockSpec((layers, width), lambda b, i: (0, 0))


def _row_of(ref, l):
    return ref.at[pl.ds(l, 1)]


def _proj_specs(B, S, l, layers):
    D = D_MODEL
    n_k = S // TK
    in_specs = [
        _vec(D, layers),
        _layer((D, D_IN), l),
        _vec(D_SGU, layers), _vec(D_SGU, layers),
        _layer((CONV_WIDTH, D_CONV), l),
        _vec(D_CONV, layers), _vec(D_CONV, layers), _vec(D_CONV, layers),
    ]
    out_specs = [
        _row(D_CONV),
        pl.BlockSpec((1, N_HEADS, V_DIM, TM), lambda b, i: (b, 0, 0, i)),
        _row(D_ATTN),
        pl.BlockSpec((1, N_HEADS, TM // TK, V_AUG, TK), lambda b, i: (b, 0, i, 0, 0)),
        _row(D_SGU),
        _row(D_SGU),
        _row(D_MIX - D_CONV),
    ]
    out_shape = [
        jax.ShapeDtypeStruct((B, S, D_CONV), BF16),
        jax.ShapeDtypeStruct((B, N_HEADS, V_DIM, S), BF16),
        jax.ShapeDtypeStruct((B, S, D_ATTN), BF16),
        jax.ShapeDtypeStruct((B, N_HEADS, n_k, V_AUG, TK), BF16),
        jax.ShapeDtypeStruct((B, S, D_SGU), F32),
        jax.ShapeDtypeStruct((B, S, D_SGU), BF16),
        jax.ShapeDtypeStruct((B, S, D_MIX - D_CONV), BF16),
    ]
    scratch = [
        pltpu.VMEM((TM + HALO, D_CONV), F32),
        pltpu.VMEM((TM, D_CONV), F32),
    ]
    return in_specs, out_specs, out_shape, scratch


def _in_proj(x, proj_params, l):
    B, S, D = x.shape
    in_specs, out_specs, out_shape, scratch = _proj_specs(B, S, l, proj_params[1].shape[0])
    return pl.pallas_call(
        functools.partial(_in_proj_kernel, l),
        grid=(B, S // TM),
        in_specs=[_row(D)] + in_specs,
        out_specs=out_specs,
        out_shape=out_shape,
        scratch_shapes=scratch,
        compiler_params=pltpu.CompilerParams(
            dimension_semantics=("arbitrary", "arbitrary"),
            vmem_limit_bytes=VMEM_LIMIT),
        name="in_proj",
    )(x, *proj_params)


def _attn_kernel(l, lam_init, q_ref, k_ref, vt_ref, pos_ref, slope_ref, lq1_ref, lk1_ref,
                 lq2_ref, lk2_ref, g_ref, o_ref, qt_ref, m_ref, acc_ref, st_ref):
    lq1_ref, lk1_ref, lq2_ref, lk2_ref, g_ref = (
        _row_of(r, l) for r in (lq1_ref, lk1_ref, lq2_ref, lk2_ref, g_ref))
    i = pl.program_id(2)
    coef = slope_ref[0][:, :1] * LOG2E

    zeros = jnp.zeros((QK_DIM, TQ), BF16)
    qt_ref[:QK_DIM, :TQ] = q_ref[0, 0, :QK_DIM, :]
    qt_ref[QK_DIM:V_DIM, :TQ] = zeros
    qt_ref[:QK_DIM, TQ:] = zeros
    qt_ref[QK_DIM:V_DIM, TQ:] = q_ref[0, 0, QK_DIM:, :]

    c0 = coef.astype(BF16).astype(F32)
    c1 = (coef - c0).astype(BF16).astype(F32)
    c2 = (coef - c0 - c1).astype(BF16).astype(F32)
    row = lax.broadcasted_iota(jnp.int32, (V_DIM, 2 * TQ), 0)
    term = row % N_COEF
    terms = jnp.where(term == 0, c0, jnp.where(term == 1, c1, c2))
    qt_ref[V_DIM:, :] = jnp.where(row < 2 * N_COEF, terms, 0.0).astype(BF16)

    m_ref[...] = jnp.full(m_ref.shape, NEG, F32)
    acc_ref[...] = jnp.zeros(acc_ref.shape, F32)

    cols = lambda s: slice(s * STRIP, (s + 1) * STRIP)

    def key_tile(j):
        kb = k_ref[0, pl.ds(pl.multiple_of(j * TK, TK), TK), :]
        return jnp.concatenate([kb, pos_ref[...]], axis=1)

    def score(kaug, columns):
        return jnp.dot(kaug, qt_ref[:, columns], preferred_element_type=F32)

    def run(tiles):
        units = []
        for j, modes in tiles:
            kaug = key_tile(j)
            c = coef * (j * TK - i * TQ).astype(F32)
            units += [(j, kaug, c, s, modes[s]) for s in range(N_STRIPS) if modes[s]]

        def qk(u):
            _, kaug, _, s, mode = units[u]
            st = st_ref.at[u % S_SLOTS]
            if mode == "diag":
                right = slice(s * STRIP + HALF, (s + 1) * STRIP)
                st[:HALF, :] = score(kaug[:HALF], cols(s))
                st[HALF:, HALF:] = score(kaug[HALF:], right)
            else:
                st[...] = score(kaug, cols(s))

        def softmax_pv(u):
            j, _, c, s, mode = units[u]
            st = st_ref.at[u % S_SLOTS]
            m_old = m_ref[:, cols(s)]
            if mode == "diag":
                kr = lax.broadcasted_iota(jnp.int32, (HALF, HALF), 0)
                qr = lax.broadcasted_iota(jnp.int32, (HALF, HALF), 1)
                top = jnp.concatenate(
                    [jnp.where(kr <= qr, st[:HALF, :HALF], NEG), st[:HALF, HALF:]], axis=1)
                bot = jnp.where(kr <= qr, st[HALF:, HALF:], NEG)
                tmax = jnp.max(top, axis=0, keepdims=True)
                bmax = jnp.max(bot, axis=0, keepdims=True)
                blk_max = jnp.concatenate(
                    [tmax[:, :HALF], jnp.maximum(tmax[:, HALF:], bmax)], axis=1)
            else:
                blk_max = jnp.max(st[...], axis=0, keepdims=True)
            m_new = jnp.maximum(m_old, blk_max + c)
            alpha = jnp.exp2(m_old - m_new)
            shift = m_new - c
            if mode == "diag":
                o = jnp.dot(vt_ref[0, 0, j, :, :HALF], jnp.exp2(top - shift).astype(BF16),
                            preferred_element_type=F32)
                o_right = jnp.dot(vt_ref[0, 0, j, :, HALF:],
                                  jnp.exp2(bot - shift[:, HALF:]).astype(BF16),
                                  preferred_element_type=F32)
                o = jnp.concatenate([o[:, :HALF], o[:, HALF:] + o_right], axis=1)
            else:
                o = None
                for h in range(TK // HALF):
                    keys = slice(h * HALF, (h + 1) * HALF)
                    p = jnp.exp2(st[keys, :] - shift).astype(BF16)
                    part = jnp.dot(vt_ref[0, 0, j, :, keys], p, preferred_element_type=F32)
                    o = part if o is None else o + part
            acc_ref[:, cols(s)] = alpha * acc_ref[:, cols(s)] + o
            m_ref[:, cols(s)] = m_new

        for t in range(-QK_AHEAD, len(units)):
            if t + QK_AHEAD < len(units):
                qk(t + QK_AHEAD)
            if t >= 0:
                softmax_pv(t)

    def body(jj, carry):
        run([(jj * Q_STRIPS + d, ["full"] * N_STRIPS) for d in range(Q_STRIPS)])
        return carry

    lax.fori_loop(0, i, body, 0)
    run([(i * Q_STRIPS + d,
          [None if s % Q_STRIPS < d else "diag" if s % Q_STRIPS == d else "full"
           for s in range(N_STRIPS)]) for d in range(Q_STRIPS)])

    lam = (jnp.exp(jnp.sum(lq1_ref[...] * lk1_ref[...], axis=-1, keepdims=True))
           - jnp.exp(jnp.sum(lq2_ref[...] * lk2_ref[...], axis=-1, keepdims=True))
           + lam_init)
    inv = 1.0 / acc_ref[V_DIM:V_DIM + 1, :]
    od = (acc_ref[:V_DIM, :TQ] * inv[:, :TQ]
          - acc_ref[:V_DIM, TQ:] * (lam * inv[:, TQ:]))
    ms = jnp.mean(od * od, axis=0, keepdims=True)
    y = (od * lax.rsqrt(ms + EPS)).T
    o_ref[0] = ((y * g_ref[...]) * (1.0 - lam_init)).astype(BF16)


def _key_positions():
    jr = np.arange(TK)
    pos = np.zeros((TK, V_DIM), np.float32)
    pos[:, :N_COEF] = (jr % POS_SPLIT)[:, None]
    pos[:, N_COEF:2 * N_COEF] = (jr - jr % POS_SPLIT)[:, None]
    return jnp.asarray(pos, dtype=BF16)


def _alibi_slopes():
    slopes = np.exp2(-8.0 * (np.arange(N_HEADS, dtype=np.float32) + 1.0) / N_HEADS)
    return jnp.asarray(np.broadcast_to(slopes[:, None, None], (N_HEADS, 1, 128)), dtype=F32)


def _attention(q, k, vt, pos, slopes, lq1, lk1, lq2, lk2, subln_g, l, lam_init):
    B, S, _ = k.shape
    n_k = S // TK
    grid = (B, N_HEADS, S // TQ)
    layers = lq1.shape[0]
    small = lambda width: pl.BlockSpec((layers, width), lambda b, h, i: (0, 0))
    return pl.pallas_call(
        functools.partial(_attn_kernel, l, lam_init),
        grid=grid,
        in_specs=[
            pl.BlockSpec((1, 1, V_DIM, TQ), lambda b, h, i: (b, h, 0, i)),
            pl.BlockSpec((1, S, V_DIM), lambda b, h, i: (b, 0, h)),
            pl.BlockSpec((1, 1, n_k, V_AUG, TK), lambda b, h, i: (b, h, 0, 0, 0)),
            pl.BlockSpec((TK, V_DIM), lambda b, h, i: (0, 0)),
            pl.BlockSpec((1, 1, 128), lambda b, h, i: (h, 0, 0)),
            small(QK_DIM), small(QK_DIM), small(QK_DIM), small(QK_DIM),
            small(V_DIM),
        ],
        out_specs=pl.BlockSpec((1, TQ, V_DIM), lambda b, h, i: (b, i, h)),
        out_shape=jax.ShapeDtypeStruct((B, S, D_ATTN), BF16),
        scratch_shapes=[
            pltpu.VMEM((2 * V_DIM, 2 * TQ), BF16),
            pltpu.VMEM((1, 2 * TQ), F32),
            pltpu.VMEM((V_AUG, 2 * TQ), F32),
            pltpu.VMEM((S_SLOTS, TK, STRIP), F32),
        ],
        compiler_params=pltpu.CompilerParams(
            dimension_semantics=("arbitrary", "arbitrary", "arbitrary"),
            vmem_limit_bytes=VMEM_LIMIT),
        name="diff_attn",
    )(q, k, vt, pos, slopes, lq1, lk1, lq2, lk2, subln_g)


def _mix_body(x_ref, ya_ref, yb_ref, u_ref, vv_ref, sgate_ref, ws_ref, bs_ref, wout_ref,
              ybuf_ref):
    ybuf_ref[:, :D_CONV] = ya_ref[0]

    ybuf_ref[:, D_CONV:D_CONV + D_ATTN] = yb_ref[0] * sgate_ref[0, :, :D_ATTN]

    tr = lax.broadcasted_iota(jnp.int32, (CHUNK, CHUNK), 0)
    tc = lax.broadcasted_iota(jnp.int32, (CHUNK, CHUNK), 1)
    wm = [jnp.where(tr >= tc, ws_ref[g], 0.0).astype(BF16) for g in range(N_GROUPS)]
    lane_group = lax.broadcasted_iota(jnp.int32, (CHUNK, D_SGU), 1) // GROUP_DIM
    for c in range(TM // CHUNK):
        rows = slice(c * CHUNK, (c + 1) * CHUNK)
        vc = vv_ref[0, rows, :]
        mixed = jnp.zeros((CHUNK, D_SGU), F32)
        for g in range(N_GROUPS):
            r = jnp.dot(wm[g], vc, preferred_element_type=F32)
            mixed = jnp.where(lane_group == g, r, mixed)
        y_c = u_ref[0, rows, :] * (mixed + bs_ref[...])
        ybuf_ref[rows, D_CONV + D_ATTN:] = (
            y_c * sgate_ref[0, rows, D_ATTN:].astype(F32)).astype(BF16)

    return x_ref[0] + jnp.dot(ybuf_ref[...], wout_ref[...].astype(BF16),
                              preferred_element_type=F32)


N_MIX_IN = 9
N_PROJ_IN = 8


def _mix_final_kernel(*refs):
    mix_in, (nf_ref, o_ref, ybuf_ref) = refs[:N_MIX_IN], refs[N_MIX_IN:]
    out = _mix_body(*mix_in, ybuf_ref)
    ms = jnp.mean(out * out, axis=-1, keepdims=True)
    o_ref[0] = (out * lax.rsqrt(ms + EPS)) * nf_ref[...]


def _mix_proj_kernel(l_proj, *refs):
    mix_in = refs[:N_MIX_IN]
    proj_in = refs[N_MIX_IN:N_MIX_IN + N_PROJ_IN]
    xo_ref, *proj_out = refs[N_MIX_IN + N_PROJ_IN:-3]
    ybuf_ref, zbuf_ref, sgate_a_ref = refs[-3:]
    out = _mix_body(*mix_in, ybuf_ref)
    xo_ref[0] = out
    _proj_body(l_proj, out, *proj_in, *proj_out, zbuf_ref, sgate_a_ref)


def _mix_specs(l):
    D = D_MODEL
    in_specs = [
        _row(D),
        _row(D_CONV),
        _row(D_ATTN),
        _row(D_SGU),
        _row(D_SGU),
        _row(D_MIX - D_CONV),
        _layer((N_GROUPS, CHUNK, CHUNK), l),
        _layer((CHUNK, D_SGU), l),
        _layer((D_MIX, D), l),
    ]
    scratch = [pltpu.VMEM((TM, D_MIX), BF16)]
    return in_specs, scratch


def _mix_final(mix_args, nf, l):
    B, S, D = mix_args[0].shape
    in_specs, scratch = _mix_specs(l)
    return pl.pallas_call(
        _mix_final_kernel,
        grid=(B, S // TM),
        in_specs=in_specs + [_vec(D, 1)],
        out_specs=_row(D),
        out_shape=jax.ShapeDtypeStruct((B, S, D), F32),
        scratch_shapes=scratch,
        compiler_params=pltpu.CompilerParams(
            dimension_semantics=("arbitrary", "arbitrary"),
            vmem_limit_bytes=VMEM_LIMIT),
        name="mix_final",
    )(*mix_args, nf)


def _mix_proj(mix_args, proj_params, l):
    B, S, D = mix_args[0].shape
    mix_in, mix_scratch = _mix_specs(l)
    proj_in, proj_out, proj_shape, proj_scratch = _proj_specs(
        B, S, l + 1, proj_params[1].shape[0])
    outs = pl.pallas_call(
        functools.partial(_mix_proj_kernel, l + 1),
        grid=(B, S // TM),
        in_specs=mix_in + proj_in,
        out_specs=[_row(D)] + proj_out,
        out_shape=[jax.ShapeDtypeStruct((B, S, D), F32)] + proj_shape,
        scratch_shapes=mix_scratch + proj_scratch,
        compiler_params=pltpu.CompilerParams(
            dimension_semantics=("arbitrary", "arbitrary"),
            vmem_limit_bytes=VMEM_LIMIT),
        name="mix_proj",
    )(*mix_args, *proj_params)
    return outs[0], outs[1:]


def kernel(x, norm_g, w_in, conv_w, conv_b, conv_ln_g, conv_ln_b, lam_q1, lam_k1, lam_q2, lam_k2,
           subln_g, sgu_ln_g, sgu_ln_b, w_s, b_s, w_out, norm_f):
    depth = w_in.shape[0]
    slopes = _alibi_slopes()
    pos = _key_positions()
    proj_params = (norm_g, w_in, sgu_ln_g, sgu_ln_b, conv_w, conv_b, conv_ln_g, conv_ln_b)
    attn_params = (lam_q1, lam_k1, lam_q2, lam_k2, subln_g)
    bs_exp = jnp.repeat(jnp.swapaxes(b_s, 1, 2), GROUP_DIM, axis=2)
    mix_params = (w_s, bs_exp, w_out)

    ya, q, k, vt, u, vv, sgate = _in_proj(x, proj_params, 0)
    for l in range(depth):
        lam_init = 0.8 - 0.6 * math.exp(-0.3 * l)
        yb = _attention(q, k, vt, pos, slopes, *attn_params, l, lam_init)
        mix_args = (x, ya, yb, u, vv, sgate) + mix_params
        if l == depth - 1:
            return _mix_final(mix_args, norm_f.reshape(1, -1), l)
        x, (ya, q, k, vt, u, vv, sgate) = _mix_proj(mix_args, proj_params, l)
```

```python
import functools
import math

import jax
import jax.numpy as jnp
import numpy as np
from jax import lax
from jax.experimental import pallas as pl
from jax.experimental.pallas import tpu as pltpu

D_MODEL = 1024
D_MIX = D_MODEL
D_CONV = D_MIX // 4
D_ATTN = D_MIX // 2
D_SGU = D_MIX // 4
N_HEADS = 4
V_DIM = D_ATTN // N_HEADS
QK_DIM = V_DIM // 2
CONV_WIDTH = 31
N_GROUPS = 4
GROUP_DIM = D_SGU // N_GROUPS
CHUNK = 128
D_IN = 2 * D_CONV + 3 * D_ATTN + 2 * D_SGU + D_MIX
EPS = 1e-6
NEG = -1e30

C_CONV = 0
C_Q = C_CONV + 2 * D_CONV
C_K = C_Q + D_ATTN
C_V = C_K + D_ATTN
C_SGU = C_V + D_ATTN
C_GATE = C_SGU + 2 * D_SGU

TM = 512
TQ = 2048
TK = 512
STRIP = TK
Q_STRIPS = TQ // STRIP
N_STRIPS = 2 * Q_STRIPS
QK_AHEAD = 4
S_SLOTS = QK_AHEAD + 1
HALF = TK // 2
SUBLANES = 8
HALO = 32
CONV_ROWS = TM // 4
V_AUG = V_DIM + 16
N_COEF = 3
POS_SPLIT = 256
LOG2E = 1.0 / math.log(2.0)
VMEM_LIMIT = 56 * 1024 * 1024

F32 = jnp.float32
BF16 = jnp.bfloat16


def _layer_norm(v, g, b):
    mu = jnp.mean(v, axis=-1, keepdims=True)
    vc = v - mu
    var = jnp.mean(vc * vc, axis=-1, keepdims=True)
    return (vc * lax.rsqrt(var + EPS)) * g + b


def _silu(v):
    return v * jax.nn.sigmoid(v)


def _causal_conv(zbuf_ref, cw_ref, bias, r0, n):
    base = HALO - (CONV_WIDTH - 1)
    conv = jnp.zeros((n, D_CONV), F32) + bias
    for b in range(SUBLANES):
        rows = n if b == 0 else n + SUBLANES
        part = None
        for w in range(CONV_WIDTH):
            if (base + w) % SUBLANES != b:
                continue
            term = zbuf_ref[pl.ds(r0 + base + w - b, rows), :] * cw_ref[w:w + 1, :]
            part = term if part is None else part + term
        conv = conv + (part if b == 0 else part[b:b + n, :])
    return conv


def _zero_after(v):
    bits = pltpu.bitcast(v[:SUBLANES, -D_CONV:], jnp.uint32)
    return ((bits >> 16) >> 16)[:1, :].astype(F32)


def _proj_body(l, x, g_ref, w_ref, slg_ref, slb_ref, cw_ref, cb_ref, clg_ref, clb_ref,
               ya_ref, q_ref, k_ref, vt_ref, u_ref, vv_ref, sgate_ref, zbuf_ref, sgate_a_ref):
    g_ref, slg_ref, slb_ref, cb_ref, clg_ref, clb_ref = (
        _row_of(r, l) for r in (g_ref, slg_ref, slb_ref, cb_ref, clg_ref, clb_ref))
    ms = jnp.mean(x * x, axis=-1, keepdims=True)
    h = ((x * lax.rsqrt(ms + EPS)) * g_ref[...]).astype(BF16)

    def proj(lo, width):
        return jnp.dot(h, w_ref[:, lo:lo + width].astype(BF16), preferred_element_type=F32)

    @pl.when(pl.program_id(1) == 0)
    def _():
        zbuf_ref[:HALO, :] = jnp.zeros((HALO, D_CONV), F32)

    def conv_chunk(c, after):
        r0, n = c * CONV_ROWS, CONV_ROWS
        conv = _causal_conv(zbuf_ref, cw_ref, cb_ref[...] + _zero_after(after), r0, n)
        y_a = _silu(_layer_norm(conv, clg_ref[...], clb_ref[...]))
        ya_ref[0, r0:r0 + n, :] = (y_a * sgate_a_ref[r0:r0 + n, :]).astype(BF16)

    pc = proj(C_CONV, 2 * D_CONV)
    zbuf_ref[HALO:, :] = pc[:, :D_CONV] * jax.nn.sigmoid(pc[:, D_CONV:])
    ga = proj(C_GATE, D_CONV)
    sgate_a_ref[...] = _silu(ga)
    conv_chunk(0, ga)
    qp = proj(C_Q, D_ATTN)
    for hd in range(N_HEADS):
        qh = qp[:, hd * V_DIM:(hd + 1) * V_DIM] * (LOG2E / math.sqrt(QK_DIM))
        q_ref[0, hd] = qh.T.astype(BF16)
    conv_chunk(1, qp)
    kp = proj(C_K, D_ATTN)
    k_ref[0] = kp.astype(BF16)
    conv_chunk(2, kp)
    v = proj(C_V, D_ATTN)
    for hd in range(N_HEADS):
        for c in range(TM // TK):
            blk = v[c * TK:(c + 1) * TK, hd * V_DIM:(hd + 1) * V_DIM]
            vt_ref[0, hd, c, :V_DIM, :] = blk.T.astype(BF16)
            vt_ref[0, hd, c, V_DIM:, :] = jnp.ones((V_AUG - V_DIM, TK), BF16)
    conv_chunk(3, v)
    ps = proj(C_SGU, 2 * D_SGU)
    u_ref[0] = ps[:, :D_SGU]
    vv_ref[0] = _layer_norm(ps[:, D_SGU:], slg_ref[...], slb_ref[...]).astype(BF16)
    sgate_ref[0] = _silu(proj(C_GATE + D_CONV, D_MIX - D_CONV)).astype(BF16)
    zbuf_ref[:HALO, :] = zbuf_ref[TM:, :]


def _in_proj_kernel(l, x_ref, *refs):
    _proj_body(l, x_ref[0], *refs)


def _row(width):
    return pl.BlockSpec((1, TM, width), lambda b, i: (b, i, 0))


def _layer(shape, l):
    return pl.BlockSpec((None,) + shape, lambda b, i: (l,) + (0,) * len(shape))


def _vec(width, layers):
    return pl.BlockSpec((layers, width), lambda b, i: (0, 0))


def _row_of(ref, l):
    return ref.at[pl.ds(l, 1)]


def _proj_specs(B, S, l, layers):
    D = D_MODEL
    n_k = S // TK
    in_specs = [
        _vec(D, layers),
        _layer((D, D_IN), l),
        _vec(D_SGU, layers), _vec(D_SGU, layers),
        _layer((CONV_WIDTH, D_CONV), l),
        _vec(D_CONV, layers), _vec(D_CONV, layers), _vec(D_CONV, layers),
    ]
    out_specs = [
        _row(D_CONV),
        pl.BlockSpec((1, N_HEADS, V_DIM, TM), lambda b, i: (b, 0, 0, i)),
        _row(D_ATTN),
        pl.BlockSpec((1, N_HEADS, TM // TK, V_AUG, TK), lambda b, i: (b, 0, i, 0, 0)),
        _row(D_SGU),
        _row(D_SGU),
        _row(D_MIX - D_CONV),
    ]
    out_shape = [
        jax.ShapeDtypeStruct((B, S, D_CONV), BF16),
        jax.ShapeDtypeStruct((B, N_HEADS, V_DIM, S), BF16),
        jax.ShapeDtypeStruct((B, S, D_ATTN), BF16),
        jax.ShapeDtypeStruct((B, N_HEADS, n_k, V_AUG, TK), BF16),
        jax.ShapeDtypeStruct((B, S, D_SGU), F32),
        jax.ShapeDtypeStruct((B, S, D_SGU), BF16),
        jax.ShapeDtypeStruct((B, S, D_MIX - D_CONV), BF16),
    ]
    scratch = [
        pltpu.VMEM((TM + HALO, D_CONV), F32),
        pltpu.VMEM((TM, D_CONV), F32),
    ]
    return in_specs, out_specs, out_shape, scratch


def _in_proj(x, proj_params, l):
    B, S, D = x.shape
    in_specs, out_specs, out_shape, scratch = _proj_specs(B, S, l, proj_params[1].shape[0])
    return pl.pallas_call(
        functools.partial(_in_proj_kernel, l),
        grid=(B, S // TM),
        in_specs=[_row(D)] + in_specs,
        out_specs=out_specs,
        out_shape=out_shape,
        scratch_shapes=scratch,
        compiler_params=pltpu.CompilerParams(
            dimension_semantics=("arbitrary", "arbitrary"),
            vmem_limit_bytes=VMEM_LIMIT),
        name="in_proj",
    )(x, *proj_params)


def _attn_kernel(l, lam_init, q_ref, k_ref, vt_ref, pos_ref, slope_ref, lq1_ref, lk1_ref,
                 lq2_ref, lk2_ref, g_ref, o_ref, qt_ref, m_ref, acc_ref, st_ref):
    lq1_ref, lk1_ref, lq2_ref, lk2_ref, g_ref = (
        _row_of(r, l) for r in (lq1_ref, lk1_ref, lq2_ref, lk2_ref, g_ref))
    i = pl.program_id(2)
    coef = slope_ref[0][:, :1] * LOG2E

    qt_ref[:QK_DIM, :TQ] = q_ref[0, 0, :QK_DIM, :]
    qt_ref[QK_DIM:V_DIM, TQ:] = q_ref[0, 0, QK_DIM:, :]

    @pl.when(i == 0)
    def _():
        zeros = jnp.zeros((QK_DIM, TQ), BF16)
        qt_ref[QK_DIM:V_DIM, :TQ] = zeros
        qt_ref[:QK_DIM, TQ:] = zeros
        c0 = coef.astype(BF16).astype(F32)
        c1 = (coef - c0).astype(BF16).astype(F32)
        c2 = (coef - c0 - c1).astype(BF16).astype(F32)
        row = lax.broadcasted_iota(jnp.int32, (V_DIM, 2 * TQ), 0)
        term = row % N_COEF
        terms = jnp.where(term == 0, c0, jnp.where(term == 1, c1, c2))
        qt_ref[V_DIM:, :] = jnp.where(row < 2 * N_COEF, terms, 0.0).astype(BF16)

    m_ref[...] = jnp.full(m_ref.shape, NEG, F32)
    acc_ref[...] = jnp.zeros(acc_ref.shape, F32)

    cols = lambda s: slice(s * STRIP, (s + 1) * STRIP)

    def key_tile(j):
        kb = k_ref[0, pl.ds(pl.multiple_of(j * TK, TK), TK), :]
        return jnp.concatenate([kb, pos_ref[...]], axis=1)

    def score(kaug, columns):
        return jnp.dot(kaug, qt_ref[:, columns], preferred_element_type=F32)

    def run(tiles):
        units = []
        for j, modes in tiles:
            kaug = key_tile(j)
            c = coef * (j * TK - i * TQ).astype(F32)
            units += [(j, kaug, c, s, modes[s]) for s in range(N_STRIPS) if modes[s]]

        def qk(u):
            _, kaug, _, s, mode = units[u]
            st = st_ref.at[u % S_SLOTS]
            if mode == "diag":
                right = slice(s * STRIP + HALF, (s + 1) * STRIP)
                st[:HALF, :] = score(kaug[:HALF], cols(s))
                st[HALF:, HALF:] = score(kaug[HALF:], right)
            else:
                st[...] = score(kaug, cols(s))

        def softmax_pv(u):
            j, _, c, s, mode = units[u]
            st = st_ref.at[u % S_SLOTS]
            m_old = m_ref[:, cols(s)]
            if mode == "diag":
                kr = lax.broadcasted_iota(jnp.int32, (HALF, HALF), 0)
                qr = lax.broadcasted_iota(jnp.int32, (HALF, HALF), 1)
                top = jnp.concatenate(
                    [jnp.where(kr <= qr, st[:HALF, :HALF], NEG), st[:HALF, HALF:]], axis=1)
                bot = jnp.where(kr <= qr, st[HALF:, HALF:], NEG)
                tmax = jnp.max(top, axis=0, keepdims=True)
                bmax = jnp.max(bot, axis=0, keepdims=True)
                blk_max = jnp.concatenate(
                    [tmax[:, :HALF], jnp.maximum(tmax[:, HALF:], bmax)], axis=1)
            else:
                blk_max = jnp.max(st[...], axis=0, keepdims=True)
            m_new = jnp.maximum(m_old, blk_max + c)
            alpha = jnp.exp2(m_old - m_new)
            shift = m_new - c
            if mode == "diag":
                o = jnp.dot(vt_ref[0, 0, j, :, :HALF], jnp.exp2(top - shift).astype(BF16),
                            preferred_element_type=F32)
                o_right = jnp.dot(vt_ref[0, 0, j, :, HALF:],
                                  jnp.exp2(bot - shift[:, HALF:]).astype(BF16),
                                  preferred_element_type=F32)
                o = jnp.concatenate([o[:, :HALF], o[:, HALF:] + o_right], axis=1)
            else:
                o = None
                for h in range(TK // HALF):
                    keys = slice(h * HALF, (h + 1) * HALF)
                    p = jnp.exp2(st[keys, :] - shift).astype(BF16)
                    part = jnp.dot(vt_ref[0, 0, j, :, keys], p, preferred_element_type=F32)
                    o = part if o is None else o + part
            acc_ref[:, cols(s)] = alpha * acc_ref[:, cols(s)] + o
            m_ref[:, cols(s)] = m_new

        for t in range(-QK_AHEAD, len(units)):
            if t + QK_AHEAD < len(units):
                qk(t + QK_AHEAD)
            if t >= 0:
                softmax_pv(t)

    def body(jj, carry):
        run([(jj * Q_STRIPS + d, ["full"] * N_STRIPS) for d in range(Q_STRIPS)])
        return carry

    lax.fori_loop(0, i, body, 0)
    run([(i * Q_STRIPS + d,
          [None if s % Q_STRIPS < d else "diag" if s % Q_STRIPS == d else "full"
           for s in range(N_STRIPS)]) for d in range(Q_STRIPS)])

    lam = (jnp.exp(jnp.sum(lq1_ref[...] * lk1_ref[...], axis=-1, keepdims=True))
           - jnp.exp(jnp.sum(lq2_ref[...] * lk2_ref[...], axis=-1, keepdims=True))
           + lam_init)
    inv = 1.0 / acc_ref[V_DIM:V_DIM + 1, :]
    od = (acc_ref[:V_DIM, :TQ] * inv[:, :TQ]
          - acc_ref[:V_DIM, TQ:] * (lam * inv[:, TQ:]))
    ms = jnp.mean(od * od, axis=0, keepdims=True)
    y = (od * lax.rsqrt(ms + EPS)).T
    o_ref[0] = ((y * g_ref[...]) * (1.0 - lam_init)).astype(BF16)


def _key_positions():
    jr = np.arange(TK)
    pos = np.zeros((TK, V_DIM), np.float32)
    pos[:, :N_COEF] = (jr % POS_SPLIT)[:, None]
    pos[:, N_COEF:2 * N_COEF] = (jr - jr % POS_SPLIT)[:, None]
    return jnp.asarray(pos, dtype=BF16)


def _alibi_slopes():
    slopes = np.exp2(-8.0 * (np.arange(N_HEADS, dtype=np.float32) + 1.0) / N_HEADS)
    return jnp.asarray(np.broadcast_to(slopes[:, None, None], (N_HEADS, 1, 128)), dtype=F32)


def _attention(q, k, vt, pos, slopes, lq1, lk1, lq2, lk2, subln_g, l, lam_init):
    B, S, _ = k.shape
    n_k = S // TK
    grid = (B, N_HEADS, S // TQ)
    layers = lq1.shape[0]
    small = lambda width: pl.BlockSpec((layers, width), lambda b, h, i: (0, 0))
    return pl.pallas_call(
        functools.partial(_attn_kernel, l, lam_init),
        grid=grid,
        in_specs=[
            pl.BlockSpec((1, 1, V_DIM, TQ), lambda b, h, i: (b, h, 0, i)),
            pl.BlockSpec((1, S, V_DIM), lambda b, h, i: (b, 0, h)),
            pl.BlockSpec((1, 1, n_k, V_AUG, TK), lambda b, h, i: (b, h, 0, 0, 0)),
            pl.BlockSpec((TK, V_DIM), lambda b, h, i: (0, 0)),
            pl.BlockSpec((1, 1, 128), lambda b, h, i: (h, 0, 0)),
            small(QK_DIM), small(QK_DIM), small(QK_DIM), small(QK_DIM),
            small(V_DIM),
        ],
        out_specs=pl.BlockSpec((1, TQ, V_DIM), lambda b, h, i: (b, i, h)),
        out_shape=jax.ShapeDtypeStruct((B, S, D_ATTN), BF16),
        scratch_shapes=[
            pltpu.VMEM((2 * V_DIM, 2 * TQ), BF16),
            pltpu.VMEM((1, 2 * TQ), F32),
            pltpu.VMEM((V_AUG, 2 * TQ), F32),
            pltpu.VMEM((S_SLOTS, TK, STRIP), F32),
        ],
        compiler_params=pltpu.CompilerParams(
            dimension_semantics=("arbitrary", "arbitrary", "arbitrary"),
            vmem_limit_bytes=VMEM_LIMIT),
        name="diff_attn",
    )(q, k, vt, pos, slopes, lq1, lk1, lq2, lk2, subln_g)


def _mix_body(x_ref, ya_ref, yb_ref, u_ref, vv_ref, sgate_ref, ws_ref, bs_ref, wout_ref,
              ybuf_ref):
    ybuf_ref[:, :D_CONV] = ya_ref[0]

    ybuf_ref[:, D_CONV:D_CONV + D_ATTN] = yb_ref[0] * sgate_ref[0, :, :D_ATTN]

    tr = lax.broadcasted_iota(jnp.int32, (CHUNK, CHUNK), 0)
    tc = lax.broadcasted_iota(jnp.int32, (CHUNK, CHUNK), 1)
    wm = [jnp.where(tr >= tc, ws_ref[g], 0.0).astype(BF16) for g in range(N_GROUPS)]
    lane_group = lax.broadcasted_iota(jnp.int32, (CHUNK, D_SGU), 1) // GROUP_DIM
    for c in range(TM // CHUNK):
        rows = slice(c * CHUNK, (c + 1) * CHUNK)
        vc = vv_ref[0, rows, :]
        mixed = jnp.zeros((CHUNK, D_SGU), F32)
        for g in range(N_GROUPS):
            r = jnp.dot(wm[g], vc, preferred_element_type=F32)
            mixed = jnp.where(lane_group == g, r, mixed)
        y_c = u_ref[0, rows, :] * (mixed + bs_ref[...])
        ybuf_ref[rows, D_CONV + D_ATTN:] = (
            y_c * sgate_ref[0, rows, D_ATTN:].astype(F32)).astype(BF16)

    return x_ref[0] + jnp.dot(ybuf_ref[...], wout_ref[...].astype(BF16),
                              preferred_element_type=F32)


N_MIX_IN = 9
N_PROJ_IN = 8


def _mix_final_kernel(*refs):
    mix_in, (nf_ref, o_ref, ybuf_ref) = refs[:N_MIX_IN], refs[N_MIX_IN:]
    out = _mix_body(*mix_in, ybuf_ref)
    ms = jnp.mean(out * out, axis=-1, keepdims=True)
    o_ref[0] = (out * lax.rsqrt(ms + EPS)) * nf_ref[...]


def _mix_proj_kernel(l_proj, *refs):
    mix_in = refs[:N_MIX_IN]
    proj_in = refs[N_MIX_IN:N_MIX_IN + N_PROJ_IN]
    xo_ref, *proj_out = refs[N_MIX_IN + N_PROJ_IN:-3]
    ybuf_ref, zbuf_ref, sgate_a_ref = refs[-3:]
    out = _mix_body(*mix_in, ybuf_ref)
    xo_ref[0] = out
    _proj_body(l_proj, out, *proj_in, *proj_out, zbuf_ref, sgate_a_ref)


def _mix_specs(l):
    D = D_MODEL
    in_specs = [
        _row(D),
        _row(D_CONV),
        _row(D_ATTN),
        _row(D_SGU),
        _row(D_SGU),
        _row(D_MIX - D_CONV),
        _layer((N_GROUPS, CHUNK, CHUNK), l),
        _layer((CHUNK, D_SGU), l),
        _layer((D_MIX, D), l),
    ]
    scratch = [pltpu.VMEM((TM, D_MIX), BF16)]
    return in_specs, scratch


def _mix_final(mix_args, nf, l):
    B, S, D = mix_args[0].shape
    in_specs, scratch = _mix_specs(l)
    return pl.pallas_call(
        _mix_final_kernel,
        grid=(B, S // TM),
        in_specs=in_specs + [_vec(D, 1)],
        out_specs=_row(D),
        out_shape=jax.ShapeDtypeStruct((B, S, D), F32),
        scratch_shapes=scratch,
        compiler_params=pltpu.CompilerParams(
            dimension_semantics=("arbitrary", "arbitrary"),
            vmem_limit_bytes=VMEM_LIMIT),
        name="mix_final",
    )(*mix_args, nf)


def _mix_proj(mix_args, proj_params, l):
    B, S, D = mix_args[0].shape
    mix_in, mix_scratch = _mix_specs(l)
    proj_in, proj_out, proj_shape, proj_scratch = _proj_specs(
        B, S, l + 1, proj_params[1].shape[0])
    outs = pl.pallas_call(
        functools.partial(_mix_proj_kernel, l + 1),
        grid=(B, S // TM),
        in_specs=mix_in + proj_in,
        out_specs=[_row(D)] + proj_out,
        out_shape=[jax.ShapeDtypeStruct((B, S, D), F32)] + proj_shape,
        scratch_shapes=mix_scratch + proj_scratch,
        compiler_params=pltpu.CompilerParams(
            dimension_semantics=("arbitrary", "arbitrary"),
            vmem_limit_bytes=VMEM_LIMIT),
        name="mix_proj",
    )(*mix_args, *proj_params)
    return outs[0], outs[1:]


def kernel(x, norm_g, w_in, conv_w, conv_b, conv_ln_g, conv_ln_b, lam_q1, lam_k1, lam_q2, lam_k2,
           subln_g, sgu_ln_g, sgu_ln_b, w_s, b_s, w_out, norm_f):
    depth = w_in.shape[0]
    slopes = _alibi_slopes()
    pos = _key_positions()
    proj_params = (norm_g, w_in, sgu_ln_g, sgu_ln_b, conv_w, conv_b, conv_ln_g, conv_ln_b)
    attn_params = (lam_q1, lam_k1, lam_q2, lam_k2, subln_g)
    bs_exp = jnp.repeat(jnp.swapaxes(b_s, 1, 2), GROUP_DIM, axis=2)
    mix_params = (w_s, bs_exp, w_out)

    ya, q, k, vt, u, vv, sgate = _in_proj(x, proj_params, 0)
    for l in range(depth):
        lam_init = 0.8 - 0.6 * math.exp(-0.3 * l)
        yb = _attention(q, k, vt, pos, slopes, *attn_params, l, lam_init)
        mix_args = (x, ya, yb, u, vv, sgate) + mix_params
        if l == depth - 1:
            return _mix_final(mix_args, norm_f.reshape(1, -1), l)
        x, (ya, q, k, vt, u, vv, sgate) = _mix_proj(mix_args, proj_params, l)
```

```python
import functools
import math

import jax
import jax.numpy as jnp
import numpy as np
from jax import lax
from jax.experimental import pallas as pl
from jax.experimental.pallas import tpu as pltpu

D_MODEL = 1024
D_MIX = D_MODEL
D_CONV = D_MIX // 4
D_ATTN = D_MIX // 2
D_SGU = D_MIX // 4
N_HEADS = 4
V_DIM = D_ATTN // N_HEADS
QK_DIM = V_DIM // 2
CONV_WIDTH = 31
N_GROUPS = 4
GROUP_DIM = D_SGU // N_GROUPS
CHUNK = 128
D_IN = 2 * D_CONV + 3 * D_ATTN + 2 * D_SGU + D_MIX
EPS = 1e-6
NEG = -1e30

C_CONV = 0
C_Q = C_CONV + 2 * D_CONV
C_K = C_Q + D_ATTN
C_V = C_K + D_ATTN
C_SGU = C_V + D_ATTN
C_GATE = C_SGU + 2 * D_SGU

TM = 512
TQ = 2048
TK = 512
STRIP = TK
Q_STRIPS = TQ // STRIP
N_STRIPS = 2 * Q_STRIPS
QK_AHEAD = 4
S_SLOTS = QK_AHEAD + 1
HALF = TK // 2
SUBLANES = 8
HALO = 32
CONV_ROWS = TM // 4
V_AUG = V_DIM + 16
N_COEF = 3
POS_SPLIT = 256
LOG2E = 1.0 / math.log(2.0)
VMEM_LIMIT = 56 * 1024 * 1024
MIX_FINAL_BUFFERS = 3

F32 = jnp.float32
BF16 = jnp.bfloat16


def _layer_norm(v, g, b):
    mu = jnp.mean(v, axis=-1, keepdims=True)
    vc = v - mu
    var = jnp.mean(vc * vc, axis=-1, keepdims=True)
    return (vc * lax.rsqrt(var + EPS)) * g + b


def _silu(v):
    return v * jax.nn.sigmoid(v)


def _causal_conv(zbuf_ref, cw_ref, bias, r0, n):
    base = HALO - (CONV_WIDTH - 1)
    conv = jnp.zeros((n, D_CONV), F32) + bias
    for b in range(SUBLANES):
        rows = n if b == 0 else n + SUBLANES
        part = None
        for w in range(CONV_WIDTH):
            if (base + w) % SUBLANES != b:
                continue
            term = zbuf_ref[pl.ds(r0 + base + w - b, rows), :] * cw_ref[w:w + 1, :]
            part = term if part is None else part + term
        conv = conv + (part if b == 0 else part[b:b + n, :])
    return conv


def _zero_after(v):
    bits = pltpu.bitcast(v[:SUBLANES, -D_CONV:], jnp.uint32)
    return ((bits >> 16) >> 16)[:1, :].astype(F32)


def _proj_body(l, x, g_ref, w_ref, slg_ref, slb_ref, cw_ref, cb_ref, clg_ref, clb_ref,
               ya_ref, q_ref, k_ref, vt_ref, u_ref, vv_ref, sgate_ref, zbuf_ref, sgate_a_ref):
    g_ref, slg_ref, slb_ref, cb_ref, clg_ref, clb_ref = (
        _row_of(r, l) for r in (g_ref, slg_ref, slb_ref, cb_ref, clg_ref, clb_ref))
    ms = jnp.mean(x * x, axis=-1, keepdims=True)
    h = ((x * lax.rsqrt(ms + EPS)) * g_ref[...]).astype(BF16)

    def proj(lo, width):
        return jnp.dot(h, w_ref[:, lo:lo + width].astype(BF16), preferred_element_type=F32)

    @pl.when(pl.program_id(1) == 0)
    def _():
        zbuf_ref[:HALO, :] = jnp.zeros((HALO, D_CONV), F32)

    def conv_chunk(c, after):
        r0, n = c * CONV_ROWS, CONV_ROWS
        conv = _causal_conv(zbuf_ref, cw_ref, cb_ref[...] + _zero_after(after), r0, n)
        y_a = _silu(_layer_norm(conv, clg_ref[...], clb_ref[...]))
        ya_ref[0, r0:r0 + n, :] = (y_a * sgate_a_ref[r0:r0 + n, :]).astype(BF16)

    pc = proj(C_CONV, 2 * D_CONV)
    zbuf_ref[HALO:, :] = pc[:, :D_CONV] * jax.nn.sigmoid(pc[:, D_CONV:])
    ga = proj(C_GATE, D_CONV)
    sgate_a_ref[...] = _silu(ga)
    conv_chunk(0, ga)
    qp = proj(C_Q, D_ATTN)
    for hd in range(N_HEADS):
        qh = qp[:, hd * V_DIM:(hd + 1) * V_DIM] * (LOG2E / math.sqrt(QK_DIM))
        q_ref[0, hd] = qh.T.astype(BF16)
    conv_chunk(1, qp)
    kp = proj(C_K, D_ATTN)
    k_ref[0] = kp.astype(BF16)
    conv_chunk(2, kp)
    v = proj(C_V, D_ATTN)
    for hd in range(N_HEADS):
        for c in range(TM // TK):
            blk = v[c * TK:(c + 1) * TK, hd * V_DIM:(hd + 1) * V_DIM]
            vt_ref[0, hd, c, :V_DIM, :] = blk.T.astype(BF16)
            vt_ref[0, hd, c, V_DIM:, :] = jnp.ones((V_AUG - V_DIM, TK), BF16)
    conv_chunk(3, v)
    ps = proj(C_SGU, 2 * D_SGU)
    u_ref[0] = ps[:, :D_SGU]
    vv_ref[0] = _layer_norm(ps[:, D_SGU:], slg_ref[...], slb_ref[...]).astype(BF16)
    sgate_ref[0] = _silu(proj(C_GATE + D_CONV, D_MIX - D_CONV)).astype(BF16)
    zbuf_ref[:HALO, :] = zbuf_ref[TM:, :]


def _in_proj_kernel(l, x_ref, *refs):
    _proj_body(l, x_ref[0], *refs)


def _row(width, buffers=None):
    mode = None if buffers is None else pl.Buffered(buffers)
    return pl.BlockSpec((1, TM, width), lambda b, i: (b, i, 0), pipeline_mode=mode)


def _layer(shape, l):
    return pl.BlockSpec((None,) + shape, lambda b, i: (l,) + (0,) * len(shape))


def _vec(width, layers):
    return pl.BlockSpec((layers, width), lambda b, i: (0, 0))


def _row_of(ref, l):
    return ref.at[pl.ds(l, 1)]


def _proj_specs(B, S, l, layers):
    D = D_MODEL
    n_k = S // TK
    in_specs = [
        _vec(D, layers),
        _layer((D, D_IN), l),
        _vec(D_SGU, layers), _vec(D_SGU, layers),
        _layer((CONV_WIDTH, D_CONV), l),
        _vec(D_CONV, layers), _vec(D_CONV, layers), _vec(D_CONV, layers),
    ]
    out_specs = [
        _row(D_CONV),
        pl.BlockSpec((1, N_HEADS, V_DIM, TM), lambda b, i: (b, 0, 0, i)),
        _row(D_ATTN),
        pl.BlockSpec((1, N_HEADS, TM // TK, V_AUG, TK), lambda b, i: (b, 0, i, 0, 0)),
        _row(D_SGU),
        _row(D_SGU),
        _row(D_MIX - D_CONV),
    ]
    out_shape = [
        jax.ShapeDtypeStruct((B, S, D_CONV), BF16),
        jax.ShapeDtypeStruct((B, N_HEADS, V_DIM, S), BF16),
        jax.ShapeDtypeStruct((B, S, D_ATTN), BF16),
        jax.ShapeDtypeStruct((B, N_HEADS, n_k, V_AUG, TK), BF16),
        jax.ShapeDtypeStruct((B, S, D_SGU), F32),
        jax.ShapeDtypeStruct((B, S, D_SGU), BF16),
        jax.ShapeDtypeStruct((B, S, D_MIX - D_CONV), BF16),
    ]
    scratch = [
        pltpu.VMEM((TM + HALO, D_CONV), F32),
        pltpu.VMEM((TM, D_CONV), F32),
    ]
    return in_specs, out_specs, out_shape, scratch


def _in_proj(x, proj_params, l):
    B, S, D = x.shape
    in_specs, out_specs, out_shape, scratch = _proj_specs(B, S, l, proj_params[1].shape[0])
    return pl.pallas_call(
        functools.partial(_in_proj_kernel, l),
        grid=(B, S // TM),
        in_specs=[_row(D)] + in_specs,
        out_specs=out_specs,
        out_shape=out_shape,
        scratch_shapes=scratch,
        compiler_params=pltpu.CompilerParams(
            dimension_semantics=("arbitrary", "arbitrary"),
            vmem_limit_bytes=VMEM_LIMIT),
        name="in_proj",
    )(x, *proj_params)


def _attn_kernel(l, lam_init, q_ref, k_ref, vt_ref, pos_ref, slope_ref, lq1_ref, lk1_ref,
                 lq2_ref, lk2_ref, g_ref, o_ref, qt_ref, m_ref, acc_ref, st_ref):
    lq1_ref, lk1_ref, lq2_ref, lk2_ref, g_ref = (
        _row_of(r, l) for r in (lq1_ref, lk1_ref, lq2_ref, lk2_ref, g_ref))
    i = pl.program_id(2)
    coef = slope_ref[0][:, :1] * LOG2E

    qt_ref[:QK_DIM, :TQ] = q_ref[0, 0, :QK_DIM, :]
    qt_ref[QK_DIM:V_DIM, TQ:] = q_ref[0, 0, QK_DIM:, :]

    @pl.when(i == 0)
    def _():
        zeros = jnp.zeros((QK_DIM, TQ), BF16)
        qt_ref[QK_DIM:V_DIM, :TQ] = zeros
        qt_ref[:QK_DIM, TQ:] = zeros
        c0 = coef.astype(BF16).astype(F32)
        c1 = (coef - c0).astype(BF16).astype(F32)
        c2 = (coef - c0 - c1).astype(BF16).astype(F32)
        row = lax.broadcasted_iota(jnp.int32, (V_DIM, 2 * TQ), 0)
        term = row % N_COEF
        terms = jnp.where(term == 0, c0, jnp.where(term == 1, c1, c2))
        qt_ref[V_DIM:, :] = jnp.where(row < 2 * N_COEF, terms, 0.0).astype(BF16)

    m_ref[...] = jnp.full(m_ref.shape, NEG, F32)
    acc_ref[...] = jnp.zeros(acc_ref.shape, F32)

    cols = lambda s: slice(s * STRIP, (s + 1) * STRIP)

    def key_tile(j):
        kb = k_ref[0, pl.ds(pl.multiple_of(j * TK, TK), TK), :]
        return jnp.concatenate([kb, pos_ref[...]], axis=1)

    def score(kaug, columns):
        return jnp.dot(kaug, qt_ref[:, columns], preferred_element_type=F32)

    def run(tiles):
        units = []
        for j, modes in tiles:
            kaug = key_tile(j)
            c = coef * (j * TK - i * TQ).astype(F32)
            units += [(j, kaug, c, s, modes[s]) for s in range(N_STRIPS) if modes[s]]

        def qk(u):
            _, kaug, _, s, mode = units[u]
            st = st_ref.at[u % S_SLOTS]
            if mode == "diag":
                right = slice(s * STRIP + HALF, (s + 1) * STRIP)
                st[:HALF, :] = score(kaug[:HALF], cols(s))
                st[HALF:, HALF:] = score(kaug[HALF:], right)
            else:
                st[...] = score(kaug, cols(s))

        def softmax_pv(u):
            j, _, c, s, mode = units[u]
            st = st_ref.at[u % S_SLOTS]
            m_old = m_ref[:, cols(s)]
            if mode == "diag":
                kr = lax.broadcasted_iota(jnp.int32, (HALF, HALF), 0)
                qr = lax.broadcasted_iota(jnp.int32, (HALF, HALF), 1)
                top = jnp.concatenate(
                    [jnp.where(kr <= qr, st[:HALF, :HALF], NEG), st[:HALF, HALF:]], axis=1)
                bot = jnp.where(kr <= qr, st[HALF:, HALF:], NEG)
                tmax = jnp.max(top, axis=0, keepdims=True)
                bmax = jnp.max(bot, axis=0, keepdims=True)
                blk_max = jnp.concatenate(
                    [tmax[:, :HALF], jnp.maximum(tmax[:, HALF:], bmax)], axis=1)
            else:
                blk_max = jnp.max(st[...], axis=0, keepdims=True)
            m_new = jnp.maximum(m_old, blk_max + c)
            alpha = jnp.exp2(m_old - m_new)
            shift = m_new - c
            if mode == "diag":
                o = jnp.dot(vt_ref[0, 0, j, :, :HALF], jnp.exp2(top - shift).astype(BF16),
                            preferred_element_type=F32)
                o_right = jnp.dot(vt_ref[0, 0, j, :, HALF:],
                                  jnp.exp2(bot - shift[:, HALF:]).astype(BF16),
                                  preferred_element_type=F32)
                o = jnp.concatenate([o[:, :HALF], o[:, HALF:] + o_right], axis=1)
            else:
                o = None
                for h in range(TK // HALF):
                    keys = slice(h * HALF, (h + 1) * HALF)
                    p = jnp.exp2(st[keys, :] - shift).astype(BF16)
                    part = jnp.dot(vt_ref[0, 0, j, :, keys], p, preferred_element_type=F32)
                    o = part if o is None else o + part
            acc_ref[:, cols(s)] = alpha * acc_ref[:, cols(s)] + o
            m_ref[:, cols(s)] = m_new

        for t in range(-QK_AHEAD, len(units)):
            if t + QK_AHEAD < len(units):
                qk(t + QK_AHEAD)
            if t >= 0:
                softmax_pv(t)

    def body(jj, carry):
        run([(jj * Q_STRIPS + d, ["full"] * N_STRIPS) for d in range(Q_STRIPS)])
        return carry

    lax.fori_loop(0, i, body, 0)
    run([(i * Q_STRIPS + d,
          [None if s % Q_STRIPS < d else "diag" if s % Q_STRIPS == d else "full"
           for s in range(N_STRIPS)]) for d in range(Q_STRIPS)])

    lam = (jnp.exp(jnp.sum(lq1_ref[...] * lk1_ref[...], axis=-1, keepdims=True))
           - jnp.exp(jnp.sum(lq2_ref[...] * lk2_ref[...], axis=-1, keepdims=True))
           + lam_init)
    inv = 1.0 / acc_ref[V_DIM:V_DIM + 1, :]
    od = (acc_ref[:V_DIM, :TQ] * inv[:, :TQ]
          - acc_ref[:V_DIM, TQ:] * (lam * inv[:, TQ:]))
    ms = jnp.mean(od * od, axis=0, keepdims=True)
    y = (od * lax.rsqrt(ms + EPS)).T
    o_ref[0] = ((y * g_ref[...]) * (1.0 - lam_init)).astype(BF16)


def _key_positions():
    jr = np.arange(TK)
    pos = np.zeros((TK, V_DIM), np.float32)
    pos[:, :N_COEF] = (jr % POS_SPLIT)[:, None]
    pos[:, N_COEF:2 * N_COEF] = (jr - jr % POS_SPLIT)[:, None]
    return jnp.asarray(pos, dtype=BF16)


def _alibi_slopes():
    slopes = np.exp2(-8.0 * (np.arange(N_HEADS, dtype=np.float32) + 1.0) / N_HEADS)
    return jnp.asarray(np.broadcast_to(slopes[:, None, None], (N_HEADS, 1, 128)), dtype=F32)


def _attention(q, k, vt, pos, slopes, lq1, lk1, lq2, lk2, subln_g, l, lam_init):
    B, S, _ = k.shape
    n_k = S // TK
    grid = (B, N_HEADS, S // TQ)
    layers = lq1.shape[0]
    small = lambda width: pl.BlockSpec((layers, width), lambda b, h, i: (0, 0))
    return pl.pallas_call(
        functools.partial(_attn_kernel, l, lam_init),
        grid=grid,
        in_specs=[
            pl.BlockSpec((1, 1, V_DIM, TQ), lambda b, h, i: (b, h, 0, i)),
            pl.BlockSpec((1, S, V_DIM), lambda b, h, i: (b, 0, h)),
            pl.BlockSpec((1, 1, n_k, V_AUG, TK), lambda b, h, i: (b, h, 0, 0, 0)),
            pl.BlockSpec((TK, V_DIM), lambda b, h, i: (0, 0)),
            pl.BlockSpec((1, 1, 128), lambda b, h, i: (h, 0, 0)),
            small(QK_DIM), small(QK_DIM), small(QK_DIM), small(QK_DIM),
            small(V_DIM),
        ],
        out_specs=pl.BlockSpec((1, TQ, V_DIM), lambda b, h, i: (b, i, h)),
        out_shape=jax.ShapeDtypeStruct((B, S, D_ATTN), BF16),
        scratch_shapes=[
            pltpu.VMEM((2 * V_DIM, 2 * TQ), BF16),
            pltpu.VMEM((1, 2 * TQ), F32),
            pltpu.VMEM((V_AUG, 2 * TQ), F32),
            pltpu.VMEM((S_SLOTS, TK, STRIP), F32),
        ],
        compiler_params=pltpu.CompilerParams(
            dimension_semantics=("arbitrary", "arbitrary", "arbitrary"),
            vmem_limit_bytes=VMEM_LIMIT),
        name="diff_attn",
    )(q, k, vt, pos, slopes, lq1, lk1, lq2, lk2, subln_g)


def _mix_body(x_ref, ya_ref, yb_ref, u_ref, vv_ref, sgate_ref, ws_ref, bs_ref, wout_ref,
              ybuf_ref):
    ybuf_ref[:, :D_CONV] = ya_ref[0]

    ybuf_ref[:, D_CONV:D_CONV + D_ATTN] = yb_ref[0] * sgate_ref[0, :, :D_ATTN]

    tr = lax.broadcasted_iota(jnp.int32, (CHUNK, CHUNK), 0)
    tc = lax.broadcasted_iota(jnp.int32, (CHUNK, CHUNK), 1)
    wm = [jnp.where(tr >= tc, ws_ref[g], 0.0).astype(BF16) for g in range(N_GROUPS)]
    lane_group = lax.broadcasted_iota(jnp.int32, (CHUNK, D_SGU), 1) // GROUP_DIM
    for c in range(TM // CHUNK):
        rows = slice(c * CHUNK, (c + 1) * CHUNK)
        vc = vv_ref[0, rows, :]
        mixed = jnp.zeros((CHUNK, D_SGU), F32)
        for g in range(N_GROUPS):
            r = jnp.dot(wm[g], vc, preferred_element_type=F32)
            mixed = jnp.where(lane_group == g, r, mixed)
        y_c = u_ref[0, rows, :] * (mixed + bs_ref[...])
        ybuf_ref[rows, D_CONV + D_ATTN:] = (
            y_c * sgate_ref[0, rows, D_ATTN:].astype(F32)).astype(BF16)

    return x_ref[0] + jnp.dot(ybuf_ref[...], wout_ref[...].astype(BF16),
                              preferred_element_type=F32)


N_MIX_IN = 9
N_PROJ_IN = 8


def _mix_final_kernel(*refs):
    mix_in, (nf_ref, o_ref, ybuf_ref) = refs[:N_MIX_IN], refs[N_MIX_IN:]
    out = _mix_body(*mix_in, ybuf_ref)
    ms = jnp.mean(out * out, axis=-1, keepdims=True)
    o_ref[0] = (out * lax.rsqrt(ms + EPS)) * nf_ref[...]


def _mix_proj_kernel(l_proj, *refs):
    mix_in = refs[:N_MIX_IN]
    proj_in = refs[N_MIX_IN:N_MIX_IN + N_PROJ_IN]
    xo_ref, *proj_out = refs[N_MIX_IN + N_PROJ_IN:-3]
    ybuf_ref, zbuf_ref, sgate_a_ref = refs[-3:]
    out = _mix_body(*mix_in, ybuf_ref)
    xo_ref[0] = out
    _proj_body(l_proj, out, *proj_in, *proj_out, zbuf_ref, sgate_a_ref)


def _mix_specs(l, buffers=None):
    D = D_MODEL
    in_specs = [
        _row(D, buffers),
        _row(D_CONV, buffers),
        _row(D_ATTN, buffers),
        _row(D_SGU, buffers),
        _row(D_SGU, buffers),
        _row(D_MIX - D_CONV, buffers),
        _layer((N_GROUPS, CHUNK, CHUNK), l),
        _layer((CHUNK, D_SGU), l),
        _layer((D_MIX, D), l),
    ]
    scratch = [pltpu.VMEM((TM, D_MIX), BF16)]
    return in_specs, scratch


def _mix_final(mix_args, nf, l):
    B, S, D = mix_args[0].shape
    in_specs, scratch = _mix_specs(l, buffers=MIX_FINAL_BUFFERS)
    in_specs = in_specs + [_vec(D, 1)]
    n_in = len(in_specs)

    def outer(*refs):
        ins, o_hbm, ybuf_ref = refs[:n_in], refs[n_in], refs[n_in + 1]

        def body(*tile_refs):
            _mix_final_kernel(*tile_refs, ybuf_ref)

        pltpu.emit_pipeline(body, grid=(B, S // TM), in_specs=in_specs,
                            out_specs=[_row(D)])(*ins, o_hbm)

    return pl.pallas_call(
        outer,
        in_specs=[pl.BlockSpec(memory_space=pl.ANY)] * n_in,
        out_specs=pl.BlockSpec(memory_space=pl.ANY),
        out_shape=jax.ShapeDtypeStruct((B, S, D), F32),
        scratch_shapes=scratch,
        compiler_params=pltpu.CompilerParams(vmem_limit_bytes=VMEM_LIMIT),
        name="mix_final",
    )(*mix_args, nf)


def _mix_proj(mix_args, proj_params, l):
    B, S, D = mix_args[0].shape
    mix_in, mix_scratch = _mix_specs(l)
    proj_in, proj_out, proj_shape, proj_scratch = _proj_specs(
        B, S, l + 1, proj_params[1].shape[0])
    outs = pl.pallas_call(
        functools.partial(_mix_proj_kernel, l + 1),
        grid=(B, S // TM),
        in_specs=mix_in + proj_in,
        out_specs=[_row(D)] + proj_out,
        out_shape=[jax.ShapeDtypeStruct((B, S, D), F32)] + proj_shape,
        scratch_shapes=mix_scratch + proj_scratch,
        compiler_params=pltpu.CompilerParams(
            dimension_semantics=("arbitrary", "arbitrary"),
            vmem_limit_bytes=VMEM_LIMIT),
        name="mix_proj",
    )(*mix_args, *proj_params)
    return outs[0], outs[1:]


def kernel(x, norm_g, w_in, conv_w, conv_b, conv_ln_g, conv_ln_b, lam_q1, lam_k1, lam_q2, lam_k2,
           subln_g, sgu_ln_g, sgu_ln_b, w_s, b_s, w_out, norm_f):
    depth = w_in.shape[0]
    slopes = _alibi_slopes()
    pos = _key_positions()
    proj_params = (norm_g, w_in, sgu_ln_g, sgu_ln_b, conv_w, conv_b, conv_ln_g, conv_ln_b)
    attn_params = (lam_q1, lam_k1, lam_q2, lam_k2, subln_g)
    bs_exp = jnp.repeat(jnp.swapaxes(b_s, 1, 2), GROUP_DIM, axis=2)
    mix_params = (w_s, bs_exp, w_out)

    ya, q, k, vt, u, vv, sgate = _in_proj(x, proj_params, 0)
    for l in range(depth):
        lam_init = 0.8 - 0.6 * math.exp(-0.3 * l)
        yb = _attention(q, k, vt, pos, slopes, *attn_params, l, lam_init)
        mix_args = (x, ya, yb, u, vv, sgate) + mix_params
        if l == depth - 1:
            return _mix_final(mix_args, norm_f.reshape(1, -1), l)
        x, (ya, q, k, vt, u, vv, sgate) = _mix_proj(mix_args, proj_params, l)
```
